```python
import jax, jax.numpy as jnp
from jax import lax
import numpy as np

D_MODEL = 1024
BATCH = 16
SEQ = 2048
DEPTH = 4

CHUNK = 64
HEAD_DIM = 64
SWA_HEADS = 8
SWA_KV_HEADS = 2
SWA_GROUP = SWA_HEADS // SWA_KV_HEADS
SWA_WINDOW = 128
WINDOW_CHUNKS = SWA_WINDOW // CHUNK
SPAN = (WINDOW_CHUNKS + 1) * CHUNK
CONV_WIDTH = 256
CONV_K = 3
MEM_HEADS = 4
MEM_LEN = 256
SWA_WIDTH = SWA_HEADS * HEAD_DIM
KV_WIDTH = SWA_KV_HEADS * HEAD_DIM
MEM_WIDTH = MEM_HEADS * HEAD_DIM
MIX_WIDTH = SWA_WIDTH + CONV_WIDTH + MEM_WIDTH
IN_WIDTH = SWA_WIDTH + 2 * KV_WIDTH + 3 * CONV_WIDTH + MEM_WIDTH
SPLIT_POINTS = (SWA_WIDTH,
                SWA_WIDTH + KV_WIDTH,
                SWA_WIDTH + 2 * KV_WIDTH,
                SWA_WIDTH + 2 * KV_WIDTH + CONV_WIDTH,
                SWA_WIDTH + 2 * KV_WIDTH + 2 * CONV_WIDTH,
                SWA_WIDTH + 2 * KV_WIDTH + 3 * CONV_WIDTH)
D_FF = 3584
N_EXPERTS = 8
TOP_K = 2
N_DENSE = (DEPTH + 1) // 2
N_MOE = DEPTH // 2
EPS = 1e-6

kernel_name = 'hymba_style_chunk_causal_swa_conv_mem_moe_trunk'


def rms_norm(x, g):
    xf = x.astype(jnp.float32)
    y = xf * lax.rsqrt(jnp.mean(xf * xf, axis=-1, keepdims=True) + EPS)
    return (y * g.astype(jnp.float32)).astype(x.dtype)


def alibi_slopes(n):
    return jnp.asarray([2.0 ** (-8.0 * (i + 1) / n) for i in range(n)], dtype=jnp.float32)


def swa_sink_attention(q, k, v, sinks):
    b, s = q.shape[0], q.shape[1]
    nc = s // CHUNK
    qc = q.reshape(b, nc, CHUNK, SWA_KV_HEADS, SWA_GROUP, HEAD_DIM)
    pad = ((0, 0), (WINDOW_CHUNKS * CHUNK, 0), (0, 0), (0, 0))

    def band(t):
        tp = jnp.pad(t, pad).reshape(b, nc + WINDOW_CHUNKS, CHUNK, SWA_KV_HEADS, HEAD_DIM)
        return jnp.concatenate([tp[:, j:j + nc] for j in range(WINDOW_CHUNKS + 1)], axis=2)

    kb, vb = band(k), band(v)
    scores = jnp.einsum('bnqkgd,bnskd->bnkgqs', qc, kb).astype(jnp.float32) * (HEAD_DIM ** -0.5)
    qi = jnp.arange(CHUNK)[:, None] + WINDOW_CHUNKS * CHUNK
    sj = jnp.arange(SPAN)[None, :]
    dist = jnp.abs(qi - sj).astype(jnp.float32)
    slopes = alibi_slopes(SWA_HEADS).reshape(SWA_KV_HEADS, SWA_GROUP)
    scores = scores - slopes[None, None, :, :, None, None] * dist[None, None, None, None]
    key_chunk = jnp.arange(nc)[:, None] - WINDOW_CHUNKS + (jnp.arange(SPAN) // CHUNK)[None, :]
    valid = key_chunk >= 0
    scores = jnp.where(valid[None, :, None, None, None, :], scores, -jnp.inf)
    sink = sinks.astype(jnp.float32).reshape(SWA_KV_HEADS, SWA_GROUP)[None, None, :, :, None, None]
    m = jnp.maximum(jnp.max(scores, axis=-1, keepdims=True), sink)
    p = jnp.exp(scores - m)
    denom = jnp.sum(p, axis=-1, keepdims=True) + jnp.exp(sink - m)
    p = (p / denom).astype(v.dtype)
    out = jnp.einsum('bnkgqs,bnskd->bnqkgd', p, vb)
    return out.reshape(b, s, SWA_WIDTH)


def gated_short_conv(gate_b, gate_c, u, w):
    s = u.shape[1]
    z = gate_c * u
    zp = jnp.pad(z, ((0, 0), (CONV_K - 1, 0), (0, 0)))
    y = w[0] * zp[:, 0:s]
    for j in range(1, CONV_K):
        y = y + w[j] * zp[:, j:j + s]
    return gate_b * y


def memory_attention(q, mk, mv):
    b, s = q.shape[0], q.shape[1]
    scores = jnp.einsum('bshd,bmhd->bhsm', q, mk).astype(jnp.float32) * (HEAD_DIM ** -0.5)
    p = jax.nn.softmax(scores, axis=-1).astype(mv.dtype)
    out = jnp.einsum('bhsm,bmhd->bshd', p, mv)
    return out.reshape(b, s, MEM_WIDTH)


def swiglu(h, wg, wu, wd):
    return (jax.nn.silu(h @ wg) * (h @ wu)) @ wd


def moe_ffn(h, w_router, b_router, wg, wu, wd):
    b, s, d = h.shape
    t = h.reshape(b * s, d)
    logits = (t @ w_router).astype(jnp.float32) + b_router.astype(jnp.float32)
    top_val, top_idx = lax.top_k(logits, TOP_K)
    gates = jax.nn.softmax(top_val, axis=-1)
    combine = jnp.sum(jax.nn.one_hot(top_idx, N_EXPERTS, dtype=jnp.float32) * gates[..., None], axis=1)
    combine = combine.astype(t.dtype)
    out = jnp.zeros_like(t)
    for e in range(N_EXPERTS):
        out = out + combine[:, e:e + 1] * swiglu(t, wg[e], wu[e], wd[e])
    return out.reshape(b, s, d)


def setup_inputs(seed: int = 0) -> dict:
    key = jax.random.key(seed)
    ks = jax.random.split(key, 32)
    f32 = jnp.float32
    res = (2 * DEPTH) ** -0.5

    def nrm(k, shape, scale):
        return jax.random.normal(k, shape, f32) * scale

    def gain(k, shape):
        return 1.0 + 0.1 * jax.random.normal(k, shape, f32)

    return {
        'x': nrm(ks[0], (BATCH, SEQ, D_MODEL), 1.0),
        'mem': nrm(ks[1], (BATCH, MEM_LEN, D_MODEL), 1.0),
        'g_mix': gain(ks[2], (DEPTH, D_MODEL)),
        'w_in': nrm(ks[3], (DEPTH, D_MODEL, IN_WIDTH), D_MODEL ** -0.5),
        'g_q_swa': gain(ks[4], (DEPTH, HEAD_DIM)),
        'g_k_swa': gain(ks[5], (DEPTH, HEAD_DIM)),
        'sinks': nrm(ks[6], (DEPTH, SWA_HEADS), 1.0),
        'conv_w': nrm(ks[7], (DEPTH, CONV_K, CONV_WIDTH), CONV_K ** -0.5),
        'g_mem': gain(ks[8], (DEPTH, D_MODEL)),
        'w_mem_kv': nrm(ks[9], (DEPTH, D_MODEL, 2 * MEM_WIDTH), D_MODEL ** -0.5),
        'g_q_mem': gain(ks[10], (DEPTH, HEAD_DIM)),
        'g_k_mem': gain(ks[11], (DEPTH, HEAD_DIM)),
        'g_out_swa': gain(ks[12], (DEPTH, SWA_WIDTH)),
        'g_out_conv': gain(ks[13], (DEPTH, CONV_WIDTH)),
        'g_out_mem': gain(ks[14], (DEPTH, MEM_WIDTH)),
        'w_out': nrm(ks[15], (DEPTH, MIX_WIDTH, D_MODEL), MIX_WIDTH ** -0.5 * res),
        'g_ffn': gain(ks[16], (DEPTH, D_MODEL)),
        'w_gate_dense': nrm(ks[17], (N_DENSE, D_MODEL, D_FF), D_MODEL ** -0.5),
        'w_up_dense': nrm(ks[18], (N_DENSE, D_MODEL, D_FF), D_MODEL ** -0.5),
        'w_down_dense': nrm(ks[19], (N_DENSE, D_FF, D_MODEL), D_FF ** -0.5 * res),
        'w_router': nrm(ks[20], (N_MOE, D_MODEL, N_EXPERTS), D_MODEL ** -0.5),
        'b_router': nrm(ks[21], (N_MOE, N_EXPERTS), 0.01),
        'w_gate_moe': nrm(ks[22], (N_MOE, N_EXPERTS, D_MODEL, D_FF), D_MODEL ** -0.5),
        'w_up_moe': nrm(ks[23], (N_MOE, N_EXPERTS, D_MODEL, D_FF), D_MODEL ** -0.5),
        'w_down_moe': nrm(ks[24], (N_MOE, N_EXPERTS, D_FF, D_MODEL), D_FF ** -0.5 * res),
    }


def reference(x, mem, g_mix, w_in, g_q_swa, g_k_swa, sinks, conv_w, g_mem, w_mem_kv,
              g_q_mem, g_k_mem, g_out_swa, g_out_conv, g_out_mem, w_out, g_ffn,
              w_gate_dense, w_up_dense, w_down_dense, w_router, b_router,
              w_gate_moe, w_up_moe, w_down_moe):
    b, s, _ = x.shape
    mlen = mem.shape[1]
    for l in range(DEPTH):
        h = rms_norm(x, g_mix[l])
        proj = h @ w_in[l]
        q_s, k_s, v_s, c_b, c_c, c_u, q_m = jnp.split(proj, SPLIT_POINTS, axis=-1)
        q_s = rms_norm(q_s.reshape(b, s, SWA_HEADS, HEAD_DIM), g_q_swa[l])
        k_s = rms_norm(k_s.reshape(b, s, SWA_KV_HEADS, HEAD_DIM), g_k_swa[l])
        v_s = v_s.reshape(b, s, SWA_KV_HEADS, HEAD_DIM)
        y_swa = swa_sink_attention(q_s, k_s, v_s, sinks[l])

        y_conv = gated_short_conv(c_b, c_c, c_u, conv_w[l])

        hm = rms_norm(mem, g_mem[l])
        mk, mv = jnp.split(hm @ w_mem_kv[l], 2, axis=-1)
        mk = rms_norm(mk.reshape(b, mlen, MEM_HEADS, HEAD_DIM), g_k_mem[l])
        mv = mv.reshape(b, mlen, MEM_HEADS, HEAD_DIM)
        q_m = rms_norm(q_m.reshape(b, s, MEM_HEADS, HEAD_DIM), g_q_mem[l])
        y_mem = memory_attention(q_m, mk, mv)

        y = jnp.concatenate([rms_norm(y_swa, g_out_swa[l]),
                             rms_norm(y_conv, g_out_conv[l]),
                             rms_norm(y_mem, g_out_mem[l])], axis=-1)
        x = x + y @ w_out[l]

        h2 = rms_norm(x, g_ffn[l])
        i = l // 2
        if l % 2 == 0:
            x = x + swiglu(h2, w_gate_dense[i], w_up_dense[i], w_down_dense[i])
        else:
            x = x + moe_ffn(h2, w_router[i], b_router[i], w_gate_moe[i], w_up_moe[i], w_down_moe[i])
    return x
```

```python
import functools

import jax
import jax.numpy as jnp
from jax import lax
from jax.experimental import pallas as pl
from jax.experimental.pallas import tpu as pltpu

F32 = jnp.float32
BF16 = jnp.bfloat16

D_MODEL = 1024
CHUNK = 64
HEAD_DIM = 64
SWA_HEADS = 8
SWA_KV_HEADS = 2
SWA_GROUP = SWA_HEADS // SWA_KV_HEADS
CONV_WIDTH = 256
CONV_K = 3
MEM_HEADS = 4
SWA_WIDTH = SWA_HEADS * HEAD_DIM
KV_WIDTH = SWA_KV_HEADS * HEAD_DIM
MEM_WIDTH = MEM_HEADS * HEAD_DIM
QK_WIDTH = SWA_WIDTH + KV_WIDTH
IN_WIDTH = SWA_WIDTH + 2 * KV_WIDTH + 3 * CONV_WIDTH + MEM_WIDTH
D_FF = 3584
N_EXPERTS = 8
EPS = 1e-6

PAIR = 2 * CHUNK
WIN = 4 * CHUNK
HALO = WIN - PAIR
CONV_HALO = 8
LANES = 128
ROUTER_PAD = LANES

VMEM_LIMIT_BYTES = 56 * 1024 * 1024

NT_DIMS = (((1,), (1,)), ((), ()))


def _rms_scale(x, width):
    return lax.rsqrt(jnp.sum(x * x, axis=-1, keepdims=True) * (1.0 / width) + EPS)


def _head_rms(t, bd_ref):
    ss = jnp.dot((t * t).astype(BF16), bd_ref[...], preferred_element_type=F32)
    return lax.rsqrt(ss * (1.0 / HEAD_DIM) + EPS)


def _memkv_kernel(mem_ref, g_ref, w_ref, gk_ref, bd_ref, mk_ref, mv_ref):
    m = mem_ref[...]
    hm = (m * _rms_scale(m, D_MODEL) * g_ref[...]).astype(BF16)
    kv = jnp.dot(hm, w_ref[...], preferred_element_type=F32)
    k = kv[:, :MEM_WIDTH]
    mk_ref[...] = (k * _head_rms(k, bd_ref) * gk_ref[...]).astype(BF16)
    mv_ref[...] = kv[:, MEM_WIDTH:].astype(BF16)


def _mem_kv(mem2d, g_mem, w_mem_kv, gk_mem, bd_mem):
    depth = g_mem.shape[0]
    rows = mem2d.shape[0]
    tr = min(rows, 512)
    out = jax.ShapeDtypeStruct((depth, rows, MEM_WIDTH), BF16)
    return pl.pallas_call(
        _memkv_kernel,
        grid=(depth, rows // tr),
        in_specs=[
            pl.BlockSpec((tr, D_MODEL), lambda l, i: (i, 0)),
            pl.BlockSpec((None, 1, D_MODEL), lambda l, i: (l, 0, 0)),
            pl.BlockSpec((None, D_MODEL, 2 * MEM_WIDTH), lambda l, i: (l, 0, 0)),
            pl.BlockSpec((None, 1, MEM_WIDTH), lambda l, i: (l, 0, 0)),
            pl.BlockSpec((MEM_WIDTH, MEM_WIDTH), lambda l, i: (0, 0)),
        ],
        out_specs=[
            pl.BlockSpec((None, tr, MEM_WIDTH), lambda l, i: (l, i, 0)),
            pl.BlockSpec((None, tr, MEM_WIDTH), lambda l, i: (l, i, 0)),
        ],
        out_shape=[out, out],
        compiler_params=pltpu.CompilerParams(
            dimension_semantics=("arbitrary", "arbitrary"), vmem_limit_bytes=VMEM_LIMIT_BYTES),
        name="mem_kv",
    )(mem2d, g_mem, w_mem_kv, gk_mem, bd_mem)


def _mixer_kernel(sinks_ref, x_ref, gmix_ref, win_ref, gqk_ref, gqm_ref, bdqk_ref, bdm_ref, bias_ref,
                  convw_ref, mk_ref, mv_ref, gout_ref, wout_ref, o_ref,
                  kbuf, vbuf, zbuf, yswa, ymem, *, ts):
    s_idx = pl.program_id(1)
    x = x_ref[...]
    h = (x * _rms_scale(x, D_MODEL) * gmix_ref[...]).astype(BF16)
    proj = jnp.dot(h, win_ref[...], preferred_element_type=F32)

    @pl.when(s_idx == 0)
    def _():
        kbuf[0:HALO, :] = jnp.zeros((HALO, KV_WIDTH), BF16)
        vbuf[0:HALO, :] = jnp.zeros((HALO, KV_WIDTH), BF16)
        zbuf[0:CONV_HALO, :] = jnp.zeros((CONV_HALO, CONV_WIDTH), F32)

    qk = proj[:, :QK_WIDTH]
    qk = qk * _head_rms(qk, bdqk_ref) * gqk_ref[...]
    q = qk[:, :SWA_WIDTH].astype(BF16)
    kbuf[HALO:HALO + ts, :] = qk[:, SWA_WIDTH:].astype(BF16)
    vbuf[HALO:HALO + ts, :] = proj[:, QK_WIDTH:QK_WIDTH + KV_WIDTH].astype(BF16)

    key_lane = lax.broadcasted_iota(jnp.int32, (1, WIN), 1)
    first_keys_valid = jnp.logical_or(key_lane >= HALO, s_idx > 0)

    for j in range(ts // PAIR):
        r0 = j * PAIR
        kwin = kbuf[r0:r0 + WIN, :]
        vwin = vbuf[r0:r0 + WIN, :]
        for kh in range(SWA_KV_HEADS):
            kk = kwin[:, kh * HEAD_DIM:(kh + 1) * HEAD_DIM]
            vv = vwin[:, kh * HEAD_DIM:(kh + 1) * HEAD_DIM]
            heads = [kh * SWA_GROUP + g for g in range(SWA_GROUP)]
            qg = jnp.concatenate(
                [q[r0:r0 + PAIR, hd * HEAD_DIM:(hd + 1) * HEAD_DIM] for hd in heads], axis=0)
            s_all = lax.dot_general(qg, kk, NT_DIMS, preferred_element_type=F32)
            ps, rden = [], []
            for g, hd in enumerate(heads):
                sg = s_all[g * PAIR:(g + 1) * PAIR] + bias_ref[hd]
                if j == 0:
                    sg = jnp.where(first_keys_valid, sg, -jnp.inf)
                sink = sinks_ref[hd]
                m = jnp.maximum(jnp.max(sg, axis=-1, keepdims=True), sink)
                p = jnp.exp(sg - m)
                den = jnp.sum(p, axis=-1, keepdims=True) + jnp.exp(sink - m)
                ps.append(p.astype(BF16))
                rden.append(1.0 / den)
            o_all = jnp.dot(jnp.concatenate(ps, axis=0), vv, preferred_element_type=F32)
            for g, hd in enumerate(heads):
                yswa[r0:r0 + PAIR, hd * HEAD_DIM:(hd + 1) * HEAD_DIM] = o_all[g * PAIR:(g + 1) * PAIR] * rden[g]

    kbuf[0:HALO, :] = kbuf[ts:ts + HALO, :]
    vbuf[0:HALO, :] = vbuf[ts:ts + HALO, :]

    c0 = QK_WIDTH + KV_WIDTH
    gate_b = proj[:, c0:c0 + CONV_WIDTH]
    z = proj[:, c0 + CONV_WIDTH:c0 + 2 * CONV_WIDTH] * proj[:, c0 + 2 * CONV_WIDTH:c0 + 3 * CONV_WIDTH]
    zbuf[CONV_HALO:CONV_HALO + ts, :] = z
    z1 = zbuf[CONV_HALO - 1:CONV_HALO - 1 + ts, :]
    z2 = zbuf[CONV_HALO - 2:CONV_HALO - 2 + ts, :]
    cw = convw_ref[...]
    y_conv = gate_b * (cw[0:1] * z2 + cw[1:2] * z1 + cw[2:3] * z)
    zbuf[0:CONV_HALO, :] = zbuf[ts:ts + CONV_HALO, :]

    qm = proj[:, IN_WIDTH - MEM_WIDTH:]
    qm = (qm * _head_rms(qm, bdm_ref) * gqm_ref[...]).astype(BF16)
    mk = mk_ref[...]
    mv = mv_ref[...]
    for hd in range(MEM_HEADS):
        sl = slice(hd * HEAD_DIM, (hd + 1) * HEAD_DIM)
        sm = lax.dot_general(qm[:, sl], mk[:, sl], NT_DIMS, preferred_element_type=F32)
        m = jnp.max(sm, axis=-1, keepdims=True)
        p = jnp.exp(sm - m)
        den = jnp.sum(p, axis=-1, keepdims=True)
        om = jnp.dot(p.astype(BF16), mv[:, sl], preferred_element_type=F32)
        ymem[:, sl] = om * (1.0 / den)

    gout = gout_ref[...]
    ys = yswa[...]
    ym = ymem[...]
    a = (ys * _rms_scale(ys, SWA_WIDTH) * gout[:, :SWA_WIDTH]).astype(BF16)
    b = (y_conv * _rms_scale(y_conv, CONV_WIDTH) * gout[:, SWA_WIDTH:SWA_WIDTH + CONV_WIDTH]).astype(BF16)
    c = (ym * _rms_scale(ym, MEM_WIDTH) * gout[:, SWA_WIDTH + CONV_WIDTH:]).astype(BF16)
    out = x + jnp.dot(a, wout_ref[0:SWA_WIDTH, :], preferred_element_type=F32)
    out = out + jnp.dot(b, wout_ref[SWA_WIDTH:SWA_WIDTH + CONV_WIDTH, :], preferred_element_type=F32)
    out = out + jnp.dot(c, wout_ref[SWA_WIDTH + CONV_WIDTH:, :], preferred_element_type=F32)
    o_ref[...] = out


def _mixer(x2d, batch, seq, sinks, gmix, w_in, gqk, gqm, bd_qk, bd_mem, bias, conv_w, mk, mv, gout, w_out):
    ts = min(seq, 512)
    nseq = seq // ts
    mem_len = mk.shape[0] // batch
    const = lambda b, s, sk: (0, 0)
    grid_spec = pltpu.PrefetchScalarGridSpec(
        num_scalar_prefetch=1,
        grid=(batch, nseq),
        in_specs=[
            pl.BlockSpec((ts, D_MODEL), lambda b, s, sk: (b * nseq + s, 0)),
            pl.BlockSpec((1, D_MODEL), const),
            pl.BlockSpec((D_MODEL, IN_WIDTH), const),
            pl.BlockSpec((1, QK_WIDTH), const),
            pl.BlockSpec((1, MEM_WIDTH), const),
            pl.BlockSpec((QK_WIDTH, QK_WIDTH), const),
            pl.BlockSpec((MEM_WIDTH, MEM_WIDTH), const),
            pl.BlockSpec((SWA_HEADS, PAIR, WIN), lambda b, s, sk: (0, 0, 0)),
            pl.BlockSpec((CONV_K, CONV_WIDTH), const),
            pl.BlockSpec((mem_len, MEM_WIDTH), lambda b, s, sk: (b, 0)),
            pl.BlockSpec((mem_len, MEM_WIDTH), lambda b, s, sk: (b, 0)),
            pl.BlockSpec((1, D_MODEL), const),
            pl.BlockSpec((D_MODEL, D_MODEL), const),
        ],
        out_specs=pl.BlockSpec((ts, D_MODEL), lambda b, s, sk: (b * nseq + s, 0)),
        scratch_shapes=[
            pltpu.VMEM((ts + HALO, KV_WIDTH), BF16),
            pltpu.VMEM((ts + HALO, KV_WIDTH), BF16),
            pltpu.VMEM((ts + CONV_HALO, CONV_WIDTH), F32),
            pltpu.VMEM((ts, SWA_WIDTH), F32),
            pltpu.VMEM((ts, MEM_WIDTH), F32),
        ],
    )
    return pl.pallas_call(
        functools.partial(_mixer_kernel, ts=ts),
        grid_spec=grid_spec,
        out_shape=jax.ShapeDtypeStruct(x2d.shape, F32),
        compiler_params=pltpu.CompilerParams(
            dimension_semantics=("arbitrary", "arbitrary"), vmem_limit_bytes=VMEM_LIMIT_BYTES),
        name="token_mixer",
    )(sinks, x2d, gmix, w_in, gqk, gqm, bd_qk, bd_mem, bias, conv_w, mk, mv, gout, w_out)


FF_CHUNK = 256


def _swiglu_accumulate(h, wg_ref, wu_ref, wd_ref, acc_ref, row_scale=None):
    tf = wg_ref.shape[-1]
    for c in range(tf // FF_CHUNK):
        sl = slice(c * FF_CHUNK, (c + 1) * FF_CHUNK)
        gate = jnp.dot(h, wg_ref[:, sl], preferred_element_type=F32)
        up = jnp.dot(h, wu_ref[:, sl], preferred_element_type=F32)
        act = (gate * jax.nn.sigmoid(gate) * up).astype(BF16)
        y = jnp.dot(act, wd_ref[sl, :], preferred_element_type=F32)
        if row_scale is not None:
            y = y * row_scale
        acc_ref[...] += y


def _dense_ffn_kernel(x_ref, g_ref, wg_ref, wu_ref, wd_ref, o_ref, h_ref):
    @pl.when(pl.program_id(1) == 0)
    def _():
        x = x_ref[...]
        h_ref[...] = (x * _rms_scale(x, D_MODEL) * g_ref[...]).astype(BF16)
        o_ref[...] = x

    _swiglu_accumulate(h_ref[...], wg_ref, wu_ref, wd_ref, o_ref)


def _dense_ffn(x2d, g, wg, wu, wd):
    n = x2d.shape[0]
    tm = min(n, 1024)
    tf = D_FF // 2
    return pl.pallas_call(
        _dense_ffn_kernel,
        grid=(n // tm, D_FF // tf),
        in_specs=[
            pl.BlockSpec((tm, D_MODEL), lambda i, f: (i, 0)),
            pl.BlockSpec((1, D_MODEL), lambda i, f: (0, 0)),
            pl.BlockSpec((D_MODEL, tf), lambda i, f: (0, f)),
            pl.BlockSpec((D_MODEL, tf), lambda i, f: (0, f)),
            pl.BlockSpec((tf, D_MODEL), lambda i, f: (f, 0)),
        ],
        out_specs=pl.BlockSpec((tm, D_MODEL), lambda i, f: (i, 0)),
        out_shape=jax.ShapeDtypeStruct(x2d.shape, F32),
        scratch_shapes=[pltpu.VMEM((tm, D_MODEL), BF16)],
        compiler_params=pltpu.CompilerParams(
            dimension_semantics=("arbitrary", "arbitrary"), vmem_limit_bytes=VMEM_LIMIT_BYTES),
        name="dense_ffn",
    )(x2d, g, wg, wu, wd)


MOE_TM = 512


def _moe_ffn_kernel(te_ref, nu_ref, x_ref, gate_ref, g_ref, wg_ref, wu_ref, wd_ref, o_ref, h_ref):
    i = pl.program_id(0)
    used = i < nu_ref[0]

    @pl.when(pl.program_id(1) == 0)
    def _():
        x = x_ref[...]
        h_ref[...] = (x * _rms_scale(x, D_MODEL) * g_ref[...]).astype(BF16)
        o_ref[...] = jnp.zeros(o_ref.shape, F32)

    @pl.when(used)
    def _():
        _swiglu_accumulate(h_ref[...], wg_ref, wu_ref, wd_ref, o_ref, row_scale=gate_ref[...])


def _moe_ffn(xs, gate_sorted, tile_expert, n_used, g, wg, wu, wd):
    p = xs.shape[0]
    tm = MOE_TM
    tf = D_FF // 2
    nf = D_FF // tf

    def f_eff(i, f, nu):
        return jnp.where(i < nu[0], f, nf - 1)

    grid_spec = pltpu.PrefetchScalarGridSpec(
        num_scalar_prefetch=2,
        grid=(p // tm, nf),
        in_specs=[
            pl.BlockSpec((tm, D_MODEL), lambda i, f, te, nu: (i, 0)),
            pl.BlockSpec((tm, 1), lambda i, f, te, nu: (i, 0)),
            pl.BlockSpec((1, D_MODEL), lambda i, f, te, nu: (0, 0)),
            pl.BlockSpec((None, D_MODEL, tf), lambda i, f, te, nu: (te[i], 0, f_eff(i, f, nu))),
            pl.BlockSpec((None, D_MODEL, tf), lambda i, f, te, nu: (te[i], 0, f_eff(i, f, nu))),
            pl.BlockSpec((None, tf, D_MODEL), lambda i, f, te, nu: (te[i], f_eff(i, f, nu), 0)),
        ],
        out_specs=pl.BlockSpec((tm, D_MODEL), lambda i, f, te, nu: (i, 0)),
        scratch_shapes=[pltpu.VMEM((tm, D_MODEL), BF16)],
    )
    return pl.pallas_call(
        _moe_ffn_kernel,
        grid_spec=grid_spec,
        out_shape=jax.ShapeDtypeStruct((p, D_MODEL), F32),
        compiler_params=pltpu.CompilerParams(
            dimension_semantics=("arbitrary", "arbitrary"), vmem_limit_bytes=VMEM_LIMIT_BYTES),
        name="moe_ffn",
    )(tile_expert, n_used, xs, gate_sorted, g, wg, wu, wd)


def _router_kernel(x_ref, g_ref, wh_ref, wl_ref, b_ref, idx_ref, gate_ref):
    x = x_ref[...]
    h = x * _rms_scale(x, D_MODEL) * g_ref[...]
    hh = h.astype(BF16)
    hl = (h - hh.astype(F32)).astype(BF16)
    wh = wh_ref[...]
    logits = (jnp.dot(hh, wh, preferred_element_type=F32)
              + jnp.dot(hl, wh, preferred_element_type=F32)
              + jnp.dot(hh, wl_ref[...], preferred_element_type=F32)) + b_ref[...]
    lane = lax.broadcasted_iota(jnp.int32, logits.shape, 1)
    v1 = jnp.max(logits, axis=-1, keepdims=True)
    i1 = jnp.min(jnp.where(logits == v1, lane, ROUTER_PAD), axis=-1, keepdims=True)
    rest = jnp.where(lane == i1, -jnp.inf, logits)
    v2 = jnp.max(rest, axis=-1, keepdims=True)
    i2 = jnp.min(jnp.where(rest == v2, lane, ROUTER_PAD), axis=-1, keepdims=True)
    e2 = jnp.exp(v2 - v1)
    den = 1.0 + e2
    idx_ref[...] = jnp.where(lane == 0, i1, jnp.where(lane == 1, i2, 0))
    gate_ref[...] = jnp.where(lane == 0, 1.0 / den, jnp.where(lane == 1, e2 / den, 0.0))


def _router(x2d, g, w_hi, w_lo, b_pad):
    n = x2d.shape[0]
    tr = min(n, 1024)
    return pl.pallas_call(
        _router_kernel,
        grid=(n // tr,),
        in_specs=[
            pl.BlockSpec((tr, D_MODEL), lambda i: (i, 0)),
            pl.BlockSpec((1, D_MODEL), lambda i: (0, 0)),
            pl.BlockSpec((D_MODEL, ROUTER_PAD), lambda i: (0, 0)),
            pl.BlockSpec((D_MODEL, ROUTER_PAD), lambda i: (0, 0)),
            pl.BlockSpec((1, ROUTER_PAD), lambda i: (0, 0)),
        ],
        out_specs=[
            pl.BlockSpec((tr, ROUTER_PAD), lambda i: (i, 0)),
            pl.BlockSpec((tr, ROUTER_PAD), lambda i: (i, 0)),
        ],
        out_shape=[jax.ShapeDtypeStruct((n, ROUTER_PAD), jnp.int32),
                   jax.ShapeDtypeStruct((n, ROUTER_PAD), F32)],
        compiler_params=pltpu.CompilerParams(
            dimension_semantics=("arbitrary",), vmem_limit_bytes=VMEM_LIMIT_BYTES),
        name="router",
    )(x2d, g, w_hi, w_lo, b_pad)


GATHER_ROWS = 2048
COMBINE_TOKENS = 256


def _row_copy(src_hbm, dst, src_row, dst_row, sem):
    return pltpu.make_async_copy(src_hbm.at[pl.ds(src_row, 1)], dst.at[pl.ds(dst_row, 1)], sem)


def _dispatch_kernel(tok_ref, x_hbm, o_hbm, sem):
    i = pl.program_id(0)
    base = i * GATHER_ROWS

    def start(r, c):
        _row_copy(x_hbm, o_hbm, tok_ref[0, r], base + r, sem).start()
        return c

    lax.fori_loop(0, GATHER_ROWS, start, 0)

    def wait(r, c):
        _row_copy(x_hbm, o_hbm, 0, base + r, sem).wait()
        return c

    lax.fori_loop(0, GATHER_ROWS, wait, 0)


def _dispatch(x2d, tok_of):
    p = tok_of.shape[0]
    steps = p // GATHER_ROWS
    return pl.pallas_call(
        _dispatch_kernel,
        grid=(steps,),
        in_specs=[
            pl.BlockSpec((None, 1, GATHER_ROWS), lambda i: (i, 0, 0), memory_space=pltpu.SMEM),
            pl.BlockSpec(memory_space=pl.ANY),
        ],
        out_specs=pl.BlockSpec(memory_space=pl.ANY),
        out_shape=jax.ShapeDtypeStruct((p, D_MODEL), F32),
        scratch_shapes=[pltpu.SemaphoreType.DMA(())],
        compiler_params=pltpu.CompilerParams(
            dimension_semantics=("arbitrary",), has_side_effects=True),
        name="moe_dispatch",
    )(tok_of.reshape(steps, 1, GATHER_ROWS), x2d)


def _combine_kernel(pos_ref, x_ref, y_hbm, o_ref, buf, sem):
    tt = COMBINE_TOKENS

    def start(r, c):
        _row_copy(y_hbm, buf.at[0], pos_ref[0, 2 * r], r, sem).start()
        _row_copy(y_hbm, buf.at[1], pos_ref[0, 2 * r + 1], r, sem).start()
        return c

    lax.fori_loop(0, tt, start, 0)

    def wait(r, c):
        _row_copy(y_hbm, buf.at[0], 0, r, sem).wait()
        _row_copy(y_hbm, buf.at[1], 0, r, sem).wait()
        return c

    lax.fori_loop(0, tt, wait, 0)
    o_ref[...] = x_ref[...] + buf[0] + buf[1]


def _combine(x2d, ys, pos):
    n = x2d.shape[0]
    tt = COMBINE_TOKENS
    steps = n // tt
    return pl.pallas_call(
        _combine_kernel,
        grid=(steps,),
        in_specs=[
            pl.BlockSpec((None, 1, 2 * tt), lambda i: (i, 0, 0), memory_space=pltpu.SMEM),
            pl.BlockSpec((tt, D_MODEL), lambda i: (i, 0)),
            pl.BlockSpec(memory_space=pl.ANY),
        ],
        out_specs=pl.BlockSpec((tt, D_MODEL), lambda i: (i, 0)),
        out_shape=jax.ShapeDtypeStruct((n, D_MODEL), F32),
        scratch_shapes=[pltpu.VMEM((2, tt, D_MODEL), F32), pltpu.SemaphoreType.DMA(())],
        compiler_params=pltpu.CompilerParams(
            dimension_semantics=("arbitrary",), vmem_limit_bytes=VMEM_LIMIT_BYTES),
        name="moe_combine",
    )(pos.reshape(steps, 1, 2 * tt), x2d, ys)


def _routing_tables(idx, gates, n):
    tm = MOE_TM
    p_max = 2 * n + N_EXPERTS * tm
    p_max = -(-p_max // GATHER_ROWS) * GATHER_ROWS
    e = idx.reshape(-1)
    onehot = (e[:, None] == jnp.arange(N_EXPERTS, dtype=jnp.int32)[None, :]).astype(jnp.int32)
    csum = jnp.cumsum(onehot, axis=0)
    rank = jnp.sum(csum * onehot, axis=1) - 1
    counts = csum[-1]
    padded = ((counts + tm - 1) // tm) * tm
    ends = jnp.cumsum(padded)
    starts = ends - padded
    pos = starts[e] + rank
    tok_of = jnp.zeros((p_max,), jnp.int32).at[pos].set(jnp.arange(2 * n, dtype=jnp.int32) // 2)
    gate_sorted = jnp.zeros((p_max,), F32).at[pos].set(gates.reshape(-1))
    n_used = (ends[-1] // tm).astype(jnp.int32)
    tiles = jnp.arange(p_max // tm, dtype=jnp.int32)
    tile_start = jnp.minimum(tiles, n_used - 1) * tm
    tile_expert = jnp.minimum(
        jnp.searchsorted(ends, tile_start, side="right").astype(jnp.int32), N_EXPERTS - 1)
    return pos.astype(jnp.int32), tok_of, gate_sorted.reshape(p_max, 1), tile_expert, n_used.reshape(1)


def _block_diag_ones(width):
    r = jnp.arange(width) // HEAD_DIM
    return (r[:, None] == r[None, :]).astype(BF16)


def _swa_bias():
    qi = jnp.arange(PAIR)[:, None]
    kj = jnp.arange(WIN)[None, :]
    dist = jnp.abs(qi + HALO - kj).astype(F32)
    kc = kj // CHUNK
    qc = qi // CHUNK
    visible = jnp.logical_and(kc >= qc, kc <= qc + 2)
    slopes = jnp.asarray([2.0 ** (-8.0 * (i + 1) / SWA_HEADS) for i in range(SWA_HEADS)], F32)
    bias = -slopes[:, None, None] * dist[None]
    return jnp.where(visible[None], bias, -jnp.inf)


def kernel(x, mem, g_mix, w_in, g_q_swa, g_k_swa, sinks, conv_w, g_mem, w_mem_kv, g_q_mem, g_k_mem,
           g_out_swa, g_out_conv, g_out_mem, w_out, g_ffn, w_gate_dense, w_up_dense, w_down_dense,
           w_router, b_router, w_gate_moe, w_up_moe, w_down_moe):
    batch, seq, _ = x.shape
    depth = g_mix.shape[0]
    n = batch * seq
    scale = HEAD_DIM ** -0.5

    bd_qk = _block_diag_ones(QK_WIDTH)
    bd_mem = _block_diag_ones(MEM_WIDTH)
    bias = _swa_bias()
    gqk = jnp.concatenate([jnp.tile(g_q_swa * scale, (1, SWA_HEADS)),
                           jnp.tile(g_k_swa, (1, SWA_KV_HEADS))], axis=1)[:, None, :]
    gqm = jnp.tile(g_q_mem * scale, (1, MEM_HEADS))[:, None, :]
    gkm = jnp.tile(g_k_mem, (1, MEM_HEADS))[:, None, :]
    gout = jnp.concatenate([g_out_swa, g_out_conv, g_out_mem], axis=1)[:, None, :]

    w_in_b = w_in.astype(BF16)
    w_out_b = w_out.astype(BF16)
    mk_all, mv_all = _mem_kv(mem.reshape(-1, D_MODEL), g_mem[:, None, :], w_mem_kv.astype(BF16), gkm, bd_mem)

    wr = jnp.pad(w_router, ((0, 0), (0, 0), (0, ROUTER_PAD - N_EXPERTS)))
    wr_hi = wr.astype(BF16)
    wr_lo = (wr - wr_hi.astype(F32)).astype(BF16)
    br = jnp.pad(b_router, ((0, 0), (0, ROUTER_PAD - N_EXPERTS)), constant_values=-jnp.inf)[:, None, :]

    xs = x.reshape(n, D_MODEL)
    for l in range(depth):
        xs = _mixer(xs, batch, seq, sinks[l], g_mix[l][None], w_in_b[l], gqk[l], gqm[l], bd_qk, bd_mem,
                    bias, conv_w[l], mk_all[l], mv_all[l], gout[l], w_out_b[l])
        i = l // 2
        gf = g_ffn[l][None]
        if l % 2 == 0:
            xs = _dense_ffn(xs, gf, w_gate_dense[i].astype(BF16), w_up_dense[i].astype(BF16),
                            w_down_dense[i].astype(BF16))
        else:
            idx_pad, gate_pad = _router(xs, gf, wr_hi[i], wr_lo[i], br[i])
            pos, tok_of, gate_sorted, tile_expert, n_used = _routing_tables(
                idx_pad[:, :2], gate_pad[:, :2], n)
            x_sorted = _dispatch(xs, tok_of)
            y_sorted = _moe_ffn(x_sorted, gate_sorted, tile_expert, n_used, gf,
                                w_gate_moe[i].astype(BF16), w_up_moe[i].astype(BF16),
                                w_down_moe[i].astype(BF16))
            xs = _combine(xs, y_sorted, pos)
    return xs.reshape(batch, seq, D_MODEL)
```

```python
import functools

import jax
import jax.numpy as jnp
from jax import lax
from jax.experimental import pallas as pl
from jax.experimental.pallas import tpu as pltpu

F32 = jnp.float32
BF16 = jnp.bfloat16

D_MODEL = 1024
CHUNK = 64
HEAD_DIM = 64
SWA_HEADS = 8
SWA_KV_HEADS = 2
SWA_GROUP = SWA_HEADS // SWA_KV_HEADS
CONV_WIDTH = 256
CONV_K = 3
MEM_HEADS = 4
SWA_WIDTH = SWA_HEADS * HEAD_DIM
KV_WIDTH = SWA_KV_HEADS * HEAD_DIM
MEM_WIDTH = MEM_HEADS * HEAD_DIM
QK_WIDTH = SWA_WIDTH + KV_WIDTH
IN_WIDTH = SWA_WIDTH + 2 * KV_WIDTH + 3 * CONV_WIDTH + MEM_WIDTH
D_FF = 3584
N_EXPERTS = 8
EPS = 1e-6

PAIR = 2 * CHUNK
WIN = 4 * CHUNK
HALO = WIN - PAIR
CONV_HALO = 8
LANES = 128
ROUTER_PAD = LANES

VMEM_LIMIT_BYTES = 56 * 1024 * 1024

NT_DIMS = (((1,), (1,)), ((), ()))


def _rms_scale(x, width):
    return lax.rsqrt(jnp.sum(x * x, axis=-1, keepdims=True) * (1.0 / width) + EPS)


def _head_rms(t, bd_ref):
    ss = jnp.dot((t * t).astype(BF16), bd_ref[...], preferred_element_type=F32)
    return lax.rsqrt(ss * (1.0 / HEAD_DIM) + EPS)


def _memkv_kernel(mem_ref, g_ref, w_ref, gk_ref, bd_ref, mk_ref, mv_ref):
    m = mem_ref[...]
    hm = (m * _rms_scale(m, D_MODEL) * g_ref[...]).astype(BF16)
    kv = jnp.dot(hm, w_ref[...], preferred_element_type=F32)
    k = kv[:, :MEM_WIDTH]
    mk_ref[...] = (k * _head_rms(k, bd_ref) * gk_ref[...]).astype(BF16)
    mv_ref[...] = kv[:, MEM_WIDTH:].astype(BF16)


def _mem_kv(mem2d, g_mem, w_mem_kv, gk_mem, bd_mem):
    depth = g_mem.shape[0]
    rows = mem2d.shape[0]
    tr = min(rows, 512)
    out = jax.ShapeDtypeStruct((depth, rows, MEM_WIDTH), BF16)
    return pl.pallas_call(
        _memkv_kernel,
        grid=(depth, rows // tr),
        in_specs=[
            pl.BlockSpec((tr, D_MODEL), lambda l, i: (i, 0)),
            pl.BlockSpec((None, 1, D_MODEL), lambda l, i: (l, 0, 0)),
            pl.BlockSpec((None, D_MODEL, 2 * MEM_WIDTH), lambda l, i: (l, 0, 0)),
            pl.BlockSpec((None, 1, MEM_WIDTH), lambda l, i: (l, 0, 0)),
            pl.BlockSpec((MEM_WIDTH, MEM_WIDTH), lambda l, i: (0, 0)),
        ],
        out_specs=[
            pl.BlockSpec((None, tr, MEM_WIDTH), lambda l, i: (l, i, 0)),
            pl.BlockSpec((None, tr, MEM_WIDTH), lambda l, i: (l, i, 0)),
        ],
        out_shape=[out, out],
        compiler_params=pltpu.CompilerParams(
            dimension_semantics=("arbitrary", "arbitrary"), vmem_limit_bytes=VMEM_LIMIT_BYTES),
        name="mem_kv",
    )(mem2d, g_mem, w_mem_kv, gk_mem, bd_mem)


def _mixer_kernel(sinks_ref, x_ref, gmix_ref, win_ref, gqk_ref, gqm_ref, bdqk_ref, bdm_ref, bias_ref,
                  convw_ref, mk_ref, mv_ref, gout_ref, wout_ref, o_ref,
                  kbuf, vbuf, zbuf, yswa, ymem, *, ts):
    s_idx = pl.program_id(1)
    x = x_ref[...]
    h = (x * _rms_scale(x, D_MODEL) * gmix_ref[...]).astype(BF16)
    proj = jnp.dot(h, win_ref[...], preferred_element_type=F32)

    @pl.when(s_idx == 0)
    def _():
        kbuf[0:HALO, :] = jnp.zeros((HALO, KV_WIDTH), BF16)
        vbuf[0:HALO, :] = jnp.zeros((HALO, KV_WIDTH), BF16)
        zbuf[0:CONV_HALO, :] = jnp.zeros((CONV_HALO, CONV_WIDTH), F32)

    qk = proj[:, :QK_WIDTH]
    qk = qk * _head_rms(qk, bdqk_ref) * gqk_ref[...]
    q = qk[:, :SWA_WIDTH].astype(BF16)
    kbuf[HALO:HALO + ts, :] = qk[:, SWA_WIDTH:].astype(BF16)
    vbuf[HALO:HALO + ts, :] = proj[:, QK_WIDTH:QK_WIDTH + KV_WIDTH].astype(BF16)

    key_lane = lax.broadcasted_iota(jnp.int32, (1, WIN), 1)
    first_keys_valid = jnp.logical_or(key_lane >= HALO, s_idx > 0)

    for j in range(ts // PAIR):
        r0 = j * PAIR
        kwin = kbuf[r0:r0 + WIN, :]
        vwin = vbuf[r0:r0 + WIN, :]
        for kh in range(SWA_KV_HEADS):
            kk = kwin[:, kh * HEAD_DIM:(kh + 1) * HEAD_DIM]
            vv = vwin[:, kh * HEAD_DIM:(kh + 1) * HEAD_DIM]
            heads = [kh * SWA_GROUP + g for g in range(SWA_GROUP)]
            qg = jnp.concatenate(
                [q[r0:r0 + PAIR, hd * HEAD_DIM:(hd + 1) * HEAD_DIM] for hd in heads], axis=0)
            s_all = lax.dot_general(qg, kk, NT_DIMS, preferred_element_type=F32)
            ps, rden = [], []
            for g, hd in enumerate(heads):
                sg = s_all[g * PAIR:(g + 1) * PAIR] + bias_ref[hd]
                if j == 0:
                    sg = jnp.where(first_keys_valid, sg, -jnp.inf)
                sink = sinks_ref[hd]
                m = jnp.maximum(jnp.max(sg, axis=-1, keepdims=True), sink)
                p = jnp.exp(sg - m)
                den = jnp.sum(p, axis=-1, keepdims=True) + jnp.exp(sink - m)
                ps.append(p.astype(BF16))
                rden.append(1.0 / den)
            o_all = jnp.dot(jnp.concatenate(ps, axis=0), vv, preferred_element_type=F32)
            for g, hd in enumerate(heads):
                yswa[r0:r0 + PAIR, hd * HEAD_DIM:(hd + 1) * HEAD_DIM] = o_all[g * PAIR:(g + 1) * PAIR] * rden[g]

    kbuf[0:HALO, :] = kbuf[ts:ts + HALO, :]
    vbuf[0:HALO, :] = vbuf[ts:ts + HALO, :]

    c0 = QK_WIDTH + KV_WIDTH
    gate_b = proj[:, c0:c0 + CONV_WIDTH]
    z = proj[:, c0 + CONV_WIDTH:c0 + 2 * CONV_WIDTH] * proj[:, c0 + 2 * CONV_WIDTH:c0 + 3 * CONV_WIDTH]
    zbuf[CONV_HALO:CONV_HALO + ts, :] = z
    z1 = zbuf[CONV_HALO - 1:CONV_HALO - 1 + ts, :]
    z2 = zbuf[CONV_HALO - 2:CONV_HALO - 2 + ts, :]
    cw = convw_ref[...]
    y_conv = gate_b * (cw[0:1] * z2 + cw[1:2] * z1 + cw[2:3] * z)
    zbuf[0:CONV_HALO, :] = zbuf[ts:ts + CONV_HALO, :]

    qm = proj[:, IN_WIDTH - MEM_WIDTH:]
    qm = (qm * _head_rms(qm, bdm_ref) * gqm_ref[...]).astype(BF16)
    mk = mk_ref[...]
    mv = mv_ref[...]
    for hd in range(MEM_HEADS):
        sl = slice(hd * HEAD_DIM, (hd + 1) * HEAD_DIM)
        sm = lax.dot_general(qm[:, sl], mk[:, sl], NT_DIMS, preferred_element_type=F32)
        m = jnp.max(sm, axis=-1, keepdims=True)
        p = jnp.exp(sm - m)
        den = jnp.sum(p, axis=-1, keepdims=True)
        om = jnp.dot(p.astype(BF16), mv[:, sl], preferred_element_type=F32)
        ymem[:, sl] = om * (1.0 / den)

    gout = gout_ref[...]
    ys = yswa[...]
    ym = ymem[...]
    a = (ys * _rms_scale(ys, SWA_WIDTH) * gout[:, :SWA_WIDTH]).astype(BF16)
    b = (y_conv * _rms_scale(y_conv, CONV_WIDTH) * gout[:, SWA_WIDTH:SWA_WIDTH + CONV_WIDTH]).astype(BF16)
    c = (ym * _rms_scale(ym, MEM_WIDTH) * gout[:, SWA_WIDTH + CONV_WIDTH:]).astype(BF16)
    out = x + jnp.dot(a, wout_ref[0:SWA_WIDTH, :], preferred_element_type=F32)
    out = out + jnp.dot(b, wout_ref[SWA_WIDTH:SWA_WIDTH + CONV_WIDTH, :], preferred_element_type=F32)
    out = out + jnp.dot(c, wout_ref[SWA_WIDTH + CONV_WIDTH:, :], preferred_element_type=F32)
    o_ref[...] = out


def _mixer(x2d, batch, seq, sinks, gmix, w_in, gqk, gqm, bd_qk, bd_mem, bias, conv_w, mk, mv, gout, w_out):
    ts = min(seq, 512)
    nseq = seq // ts
    mem_len = mk.shape[0] // batch
    const = lambda b, s, sk: (0, 0)
    grid_spec = pltpu.PrefetchScalarGridSpec(
        num_scalar_prefetch=1,
        grid=(batch, nseq),
        in_specs=[
            pl.BlockSpec((ts, D_MODEL), lambda b, s, sk: (b * nseq + s, 0)),
            pl.BlockSpec((1, D_MODEL), const),
            pl.BlockSpec((D_MODEL, IN_WIDTH), const),
            pl.BlockSpec((1, QK_WIDTH), const),
            pl.BlockSpec((1, MEM_WIDTH), const),
            pl.BlockSpec((QK_WIDTH, QK_WIDTH), const),
            pl.BlockSpec((MEM_WIDTH, MEM_WIDTH), const),
            pl.BlockSpec((SWA_HEADS, PAIR, WIN), lambda b, s, sk: (0, 0, 0)),
            pl.BlockSpec((CONV_K, CONV_WIDTH), const),
            pl.BlockSpec((mem_len, MEM_WIDTH), lambda b, s, sk: (b, 0)),
            pl.BlockSpec((mem_len, MEM_WIDTH), lambda b, s, sk: (b, 0)),
            pl.BlockSpec((1, D_MODEL), const),
            pl.BlockSpec((D_MODEL, D_MODEL), const),
        ],
        out_specs=pl.BlockSpec((ts, D_MODEL), lambda b, s, sk: (b * nseq + s, 0)),
        scratch_shapes=[
            pltpu.VMEM((ts + HALO, KV_WIDTH), BF16),
            pltpu.VMEM((ts + HALO, KV_WIDTH), BF16),
            pltpu.VMEM((ts + CONV_HALO, CONV_WIDTH), F32),
            pltpu.VMEM((ts, SWA_WIDTH), F32),
            pltpu.VMEM((ts, MEM_WIDTH), F32),
        ],
    )
    return pl.pallas_call(
        functools.partial(_mixer_kernel, ts=ts),
        grid_spec=grid_spec,
        out_shape=jax.ShapeDtypeStruct(x2d.shape, F32),
        compiler_params=pltpu.CompilerParams(
            dimension_semantics=("arbitrary", "arbitrary"), vmem_limit_bytes=VMEM_LIMIT_BYTES),
        name="token_mixer",
    )(sinks, x2d, gmix, w_in, gqk, gqm, bd_qk, bd_mem, bias, conv_w, mk, mv, gout, w_out)


FF_CHUNK = 256


def _swiglu_accumulate(h, wg_ref, wu_ref, wd_ref, acc_ref):
    tf = wg_ref.shape[-1]
    for c in range(tf // FF_CHUNK):
        sl = slice(c * FF_CHUNK, (c + 1) * FF_CHUNK)
        gate = jnp.dot(h, wg_ref[:, sl], preferred_element_type=F32)
        up = jnp.dot(h, wu_ref[:, sl], preferred_element_type=F32)
        act = (gate * jax.nn.sigmoid(gate) * up).astype(BF16)
        acc_ref[...] += jnp.dot(act, wd_ref[sl, :], preferred_element_type=F32)


def _dense_ffn_kernel(x_ref, g_ref, wg_ref, wu_ref, wd_ref, o_ref, h_ref):
    @pl.when(pl.program_id(1) == 0)
    def _():
        x = x_ref[...]
        h_ref[...] = (x * _rms_scale(x, D_MODEL) * g_ref[...]).astype(BF16)
        o_ref[...] = x

    _swiglu_accumulate(h_ref[...], wg_ref, wu_ref, wd_ref, o_ref)


def _dense_ffn(x2d, g, wg, wu, wd):
    n = x2d.shape[0]
    tm = min(n, 1024)
    tf = D_FF // 2
    return pl.pallas_call(
        _dense_ffn_kernel,
        grid=(n // tm, D_FF // tf),
        in_specs=[
            pl.BlockSpec((tm, D_MODEL), lambda i, f: (i, 0)),
            pl.BlockSpec((1, D_MODEL), lambda i, f: (0, 0)),
            pl.BlockSpec((D_MODEL, tf), lambda i, f: (0, f)),
            pl.BlockSpec((D_MODEL, tf), lambda i, f: (0, f)),
            pl.BlockSpec((tf, D_MODEL), lambda i, f: (f, 0)),
        ],
        out_specs=pl.BlockSpec((tm, D_MODEL), lambda i, f: (i, 0)),
        out_shape=jax.ShapeDtypeStruct(x2d.shape, F32),
        scratch_shapes=[pltpu.VMEM((tm, D_MODEL), BF16)],
        compiler_params=pltpu.CompilerParams(
            dimension_semantics=("arbitrary", "arbitrary"), vmem_limit_bytes=VMEM_LIMIT_BYTES),
        name="dense_ffn",
    )(x2d, g, wg, wu, wd)


MOE_TT = 256
MOE_TM = 512
ROW_ALIGN = 16
MOE_WIN = MOE_TT + ROW_ALIGN
ROUTE_ROWS = 8


def _moe_ffn_kernel(te_ref, tb_ref, nu_ref, x_ref, wg_ref, wu_ref, wd_ref, zeros_hbm, o_ref, acc_ref):
    del te_ref, tb_ref, zeros_hbm
    f = pl.program_id(1)
    used = pl.program_id(0) < nu_ref[0]

    @pl.when(jnp.logical_and(used, f == 0))
    def _():
        acc_ref[...] = jnp.zeros(acc_ref.shape, F32)

    @pl.when(used)
    def _():
        _swiglu_accumulate(x_ref[...], wg_ref, wu_ref, wd_ref, acc_ref)

    @pl.when(jnp.logical_and(used, f == pl.num_programs(1) - 1))
    def _():
        o_ref[...] = acc_ref[...].astype(BF16)


def _moe_ffn(xs, tile_expert, tile_block, n_used, wg, wu, wd, zeros):
    p = xs.shape[0]
    tm = MOE_TM
    tf = D_FF // 2
    nf = D_FF // tf
    max_tiles = tile_expert.shape[0]

    def f_eff(i, f, nu):
        return jnp.where(i < nu[0], f, nf - 1)

    grid_spec = pltpu.PrefetchScalarGridSpec(
        num_scalar_prefetch=3,
        grid=(max_tiles, nf),
        in_specs=[
            pl.BlockSpec((tm, D_MODEL), lambda i, f, te, tb, nu: (tb[i], 0)),
            pl.BlockSpec((None, D_MODEL, tf), lambda i, f, te, tb, nu: (te[i], 0, f_eff(i, f, nu))),
            pl.BlockSpec((None, D_MODEL, tf), lambda i, f, te, tb, nu: (te[i], 0, f_eff(i, f, nu))),
            pl.BlockSpec((None, tf, D_MODEL), lambda i, f, te, tb, nu: (te[i], f_eff(i, f, nu), 0)),
            pl.BlockSpec(memory_space=pl.ANY),
        ],
        out_specs=pl.BlockSpec((tm, D_MODEL), lambda i, f, te, tb, nu: (tb[i], 0)),
        scratch_shapes=[pltpu.VMEM((tm, D_MODEL), F32)],
    )
    return pl.pallas_call(
        _moe_ffn_kernel,
        grid_spec=grid_spec,
        out_shape=jax.ShapeDtypeStruct((p, D_MODEL), BF16),
        input_output_aliases={7: 0},
        compiler_params=pltpu.CompilerParams(
            dimension_semantics=("arbitrary", "arbitrary"), vmem_limit_bytes=VMEM_LIMIT_BYTES),
        name="moe_ffn",
    )(tile_expert, tile_block, n_used, xs, wg, wu, wd, zeros)


def _router_kernel(x_ref, g_ref, wh_ref, wl_ref, b_ref, ltri_ref, hn_ref, rc_ref, rt_ref, cnt_ref):
    x = x_ref[...]
    h = x * _rms_scale(x, D_MODEL) * g_ref[...]
    hh = h.astype(BF16)
    hn_ref[...] = hh
    hl = (h - hh.astype(F32)).astype(BF16)
    wh = wh_ref[...]
    logits = (jnp.dot(hh, wh, preferred_element_type=F32)
              + jnp.dot(hl, wh, preferred_element_type=F32)
              + jnp.dot(hh, wl_ref[...], preferred_element_type=F32)) + b_ref[...]
    lane = lax.broadcasted_iota(jnp.int32, logits.shape, 1)
    v1 = jnp.max(logits, axis=-1, keepdims=True)
    i1 = jnp.min(jnp.where(logits == v1, lane, ROUTER_PAD), axis=-1, keepdims=True)
    rest = jnp.where(lane == i1, -jnp.inf, logits)
    v2 = jnp.max(rest, axis=-1, keepdims=True)
    i2 = jnp.min(jnp.where(rest == v2, lane, ROUTER_PAD), axis=-1, keepdims=True)
    e2 = jnp.exp(v2 - v1)
    den = 1.0 + e2
    chosen = jnp.logical_or(lane == i1, lane == i2)
    cum = jnp.dot(ltri_ref[...], jnp.where(chosen, 1.0, 0.0).astype(BF16), preferred_element_type=F32)
    r1 = jnp.sum(jnp.where(lane == i1, cum, 0.0), axis=-1, keepdims=True) - 1.0
    r2 = jnp.sum(jnp.where(lane == i2, cum, 0.0), axis=-1, keepdims=True) - 1.0
    rc = jnp.zeros(logits.shape, F32)
    for k, col in enumerate((i1.astype(F32), i2.astype(F32), r1, r2, 1.0 / den, e2 / den)):
        rc = jnp.where(lane == k, col, rc)
    rc_ref[...] = rc
    rt_ref[...] = rc.T[:ROUTE_ROWS, :]
    cnt_ref[...] = cum[MOE_TT - 1:MOE_TT, :].astype(jnp.int32)


def _router(x2d, g, w_hi, w_lo, b_pad, ltri):
    n = x2d.shape[0]
    tt = MOE_TT
    nt = n // tt
    return pl.pallas_call(
        _router_kernel,
        grid=(nt,),
        in_specs=[
            pl.BlockSpec((tt, D_MODEL), lambda i: (i, 0)),
            pl.BlockSpec((1, D_MODEL), lambda i: (0, 0)),
            pl.BlockSpec((D_MODEL, ROUTER_PAD), lambda i: (0, 0)),
            pl.BlockSpec((D_MODEL, ROUTER_PAD), lambda i: (0, 0)),
            pl.BlockSpec((1, ROUTER_PAD), lambda i: (0, 0)),
            pl.BlockSpec((tt, tt), lambda i: (0, 0)),
        ],
        out_specs=[
            pl.BlockSpec((tt, D_MODEL), lambda i: (i, 0)),
            pl.BlockSpec((tt, ROUTER_PAD), lambda i: (i, 0)),
            pl.BlockSpec((ROUTE_ROWS, tt), lambda i: (0, i)),
            pl.BlockSpec((None, 1, ROUTER_PAD), lambda i: (i, 0, 0)),
        ],
        out_shape=[jax.ShapeDtypeStruct((n, D_MODEL), BF16),
                   jax.ShapeDtypeStruct((n, ROUTER_PAD), F32),
                   jax.ShapeDtypeStruct((ROUTE_ROWS, n), F32),
                   jax.ShapeDtypeStruct((nt, 1, ROUTER_PAD), jnp.int32)],
        compiler_params=pltpu.CompilerParams(
            dimension_semantics=("arbitrary",), vmem_limit_bytes=VMEM_LIMIT_BYTES),
        name="router",
    )(x2d, g, w_hi, w_lo, b_pad, ltri)


def _routing_tables(cnt, n):
    nt = n // MOE_TT
    cnt = cnt.reshape(nt, ROUTER_PAD)[:, :N_EXPERTS]
    total = jnp.sum(cnt, axis=0)
    region = ((total + MOE_WIN + MOE_TM - 1) // MOE_TM) * MOE_TM
    start = jnp.cumsum(region) - region
    first = start[None, :] + jnp.cumsum(cnt, axis=0) - cnt
    a = jnp.concatenate([first, (start + total)[None, :]], axis=0).reshape(-1).astype(jnp.int32)
    ntile = (total + MOE_TM - 1) // MOE_TM
    tend = jnp.cumsum(ntile)
    n_used = tend[-1]
    max_tiles = 2 * n // MOE_TM + N_EXPERTS
    tile = jnp.minimum(jnp.arange(max_tiles, dtype=jnp.int32), n_used - 1)
    tile_expert = jnp.minimum(jnp.sum((tend[None, :] <= tile[:, None]).astype(jnp.int32), axis=1), N_EXPERTS - 1)
    tile_block = start[tile_expert] // MOE_TM + tile - (tend - ntile)[tile_expert]
    return a, tile_expert.astype(jnp.int32), tile_block.astype(jnp.int32), n_used.reshape(1).astype(jnp.int32)


def _sorted_rows(n):
    return -(-(2 * n + N_EXPERTS * (MOE_WIN + MOE_TM)) // MOE_TM) * MOE_TM


def _window(a_ref, t, e):
    a = a_ref[t * N_EXPERTS + e]
    off = jnp.bitwise_and(a, ROW_ALIGN - 1)
    return pl.multiple_of(a - off, ROW_ALIGN), off


def _expert_slot(e, e1, e2, r1, r2, off):
    d = jnp.where(e1 == e, r1, jnp.where(e2 == e, r2, -1.0))
    return jnp.where(d >= 0.0, d + off.astype(F32), -1.0).astype(jnp.int32)


def _dispatch_kernel(a_ref, hn_ref, rt_ref, zeros_hbm, xs_hbm, carry, stage, sem):
    del zeros_hbm
    t = pl.program_id(0)
    nt = pl.num_programs(0)
    slot = t % 2

    def window_copy(sl, e, base):
        return pltpu.make_async_copy(stage.at[sl, e], xs_hbm.at[pl.ds(base, MOE_WIN)], sem)

    @pl.when(t == 0)
    def _():
        carry[...] = jnp.zeros(carry.shape, F32)

    hn = hn_ref[...]
    rt = rt_ref[...]
    e1, e2, r1, r2 = rt[0:1], rt[1:2], rt[2:3], rt[3:4]
    row = lax.broadcasted_iota(jnp.int32, (MOE_WIN, MOE_TT), 0)
    head_row = lax.broadcasted_iota(jnp.int32, (ROW_ALIGN, 1), 0)
    bases = []
    for e in range(N_EXPERTS):
        base, off = _window(a_ref, t, e)
        bases.append(base)
        slot_of = _expert_slot(e, e1, e2, r1, r2, off)
        sel = jnp.where(row == slot_of, 1.0, 0.0).astype(BF16)
        rows = jnp.dot(sel, hn, preferred_element_type=F32)
        head = jnp.where(head_row < off, carry[e], rows[0:ROW_ALIGN])
        stage[slot, e, 0:ROW_ALIGN, :] = head.astype(BF16)
        stage[slot, e, ROW_ALIGN:, :] = rows[ROW_ALIGN:].astype(BF16)
        filled = off + a_ref[(t + 1) * N_EXPERTS + e] - a_ref[t * N_EXPERTS + e]
        last_group = pl.multiple_of(lax.shift_right_logical(filled, 4) * ROW_ALIGN, ROW_ALIGN)
        carry[e] = stage[slot, e, pl.ds(last_group, ROW_ALIGN), :].astype(F32)

    @pl.when(t > 0)
    def _():
        for e in range(N_EXPERTS):
            window_copy(1 - slot, e, 0).wait()

    for e in range(N_EXPERTS):
        window_copy(slot, e, bases[e]).start()

    @pl.when(t == nt - 1)
    def _():
        for e in range(N_EXPERTS):
            window_copy(slot, e, 0).wait()


def _dispatch(hn, route_t, a, zeros):
    n = hn.shape[0]
    tt = MOE_TT
    grid_spec = pltpu.PrefetchScalarGridSpec(
        num_scalar_prefetch=1,
        grid=(n // tt,),
        in_specs=[
            pl.BlockSpec((tt, D_MODEL), lambda t, a: (t, 0)),
            pl.BlockSpec((ROUTE_ROWS, tt), lambda t, a: (0, t)),
            pl.BlockSpec(memory_space=pl.ANY),
        ],
        out_specs=pl.BlockSpec(memory_space=pl.ANY),
        scratch_shapes=[
            pltpu.VMEM((N_EXPERTS, ROW_ALIGN, D_MODEL), F32),
            pltpu.VMEM((2, N_EXPERTS, MOE_WIN, D_MODEL), BF16),
            pltpu.SemaphoreType.DMA(()),
        ],
    )
    return pl.pallas_call(
        _dispatch_kernel,
        grid_spec=grid_spec,
        out_shape=jax.ShapeDtypeStruct(zeros.shape, BF16),
        input_output_aliases={3: 0},
        compiler_params=pltpu.CompilerParams(
            dimension_semantics=("arbitrary",), vmem_limit_bytes=VMEM_LIMIT_BYTES),
        name="moe_dispatch",
    )(a, hn, route_t, zeros)


def _combine_kernel(a_ref, x_ref, rc_ref, ys_hbm, o_ref, ybuf, sem):
    t = pl.program_id(0)
    nt = pl.num_programs(0)
    slot = t % 2

    def window_copy(tile, sl, e):
        base, _ = _window(a_ref, tile, e)
        return pltpu.make_async_copy(ys_hbm.at[pl.ds(base, MOE_WIN)], ybuf.at[sl, e], sem.at[sl])

    @pl.when(t == 0)
    def _():
        for e in range(N_EXPERTS):
            window_copy(0, 0, e).start()

    @pl.when(t + 1 < nt)
    def _():
        for e in range(N_EXPERTS):
            window_copy(t + 1, 1 - slot, e).start()

    for e in range(N_EXPERTS):
        window_copy(t, slot, e).wait()

    rc = rc_ref[...]
    e1, e2, r1, r2, g1, g2 = (rc[:, k:k + 1] for k in range(6))
    lane = lax.broadcasted_iota(jnp.int32, (MOE_TT, MOE_WIN), 1)
    acc = jnp.zeros((MOE_TT, D_MODEL), F32)
    for e in range(N_EXPERTS):
        _, off = _window(a_ref, t, e)
        slot_of = _expert_slot(e, e1, e2, r1, r2, off)
        sel = jnp.where(lane == slot_of, 1.0, 0.0).astype(BF16)
        gate = jnp.where(e1 == e, g1, jnp.where(e2 == e, g2, 0.0))
        acc = acc + gate * jnp.dot(sel, ybuf[slot, e], preferred_element_type=F32)
    o_ref[...] = x_ref[...] + acc


def _combine(x2d, route_c, ys, a):
    n = x2d.shape[0]
    tt = MOE_TT
    grid_spec = pltpu.PrefetchScalarGridSpec(
        num_scalar_prefetch=1,
        grid=(n // tt,),
        in_specs=[
            pl.BlockSpec((tt, D_MODEL), lambda t, a: (t, 0)),
            pl.BlockSpec((tt, ROUTER_PAD), lambda t, a: (t, 0)),
            pl.BlockSpec(memory_space=pl.ANY),
        ],
        out_specs=pl.BlockSpec((tt, D_MODEL), lambda t, a: (t, 0)),
        scratch_shapes=[
            pltpu.VMEM((2, N_EXPERTS, MOE_WIN, D_MODEL), BF16),
            pltpu.SemaphoreType.DMA((2,)),
        ],
    )
    return pl.pallas_call(
        _combine_kernel,
        grid_spec=grid_spec,
        out_shape=jax.ShapeDtypeStruct((n, D_MODEL), F32),
        compiler_params=pltpu.CompilerParams(
            dimension_semantics=("arbitrary",), vmem_limit_bytes=VMEM_LIMIT_BYTES),
        name="moe_combine",
    )(a, x2d, route_c, ys)


def _block_diag_ones(width):
    r = jnp.arange(width) // HEAD_DIM
    return (r[:, None] == r[None, :]).astype(BF16)


def _swa_bias():
    qi = jnp.arange(PAIR)[:, None]
    kj = jnp.arange(WIN)[None, :]
    dist = jnp.abs(qi + HALO - kj).astype(F32)
    kc = kj // CHUNK
    qc = qi // CHUNK
    visible = jnp.logical_and(kc >= qc, kc <= qc + 2)
    slopes = jnp.asarray([2.0 ** (-8.0 * (i + 1) / SWA_HEADS) for i in range(SWA_HEADS)], F32)
    bias = -slopes[:, None, None] * dist[None]
    return jnp.where(visible[None], bias, -jnp.inf)


def kernel(x, mem, g_mix, w_in, g_q_swa, g_k_swa, sinks, conv_w, g_mem, w_mem_kv, g_q_mem, g_k_mem,
           g_out_swa, g_out_conv, g_out_mem, w_out, g_ffn, w_gate_dense, w_up_dense, w_down_dense,
           w_router, b_router, w_gate_moe, w_up_moe, w_down_moe):
    batch, seq, _ = x.shape
    depth = g_mix.shape[0]
    n = batch * seq
    scale = HEAD_DIM ** -0.5

    bd_qk = _block_diag_ones(QK_WIDTH)
    bd_mem = _block_diag_ones(MEM_WIDTH)
    bias = _swa_bias()
    gqk = jnp.concatenate([jnp.tile(g_q_swa * scale, (1, SWA_HEADS)),
                           jnp.tile(g_k_swa, (1, SWA_KV_HEADS))], axis=1)[:, None, :]
    gqm = jnp.tile(g_q_mem * scale, (1, MEM_HEADS))[:, None, :]
    gkm = jnp.tile(g_k_mem, (1, MEM_HEADS))[:, None, :]
    gout = jnp.concatenate([g_out_swa, g_out_conv, g_out_mem], axis=1)[:, None, :]

    w_in_b = w_in.astype(BF16)
    w_out_b = w_out.astype(BF16)
    mk_all, mv_all = _mem_kv(mem.reshape(-1, D_MODEL), g_mem[:, None, :], w_mem_kv.astype(BF16), gkm, bd_mem)

    wr = jnp.pad(w_router, ((0, 0), (0, 0), (0, ROUTER_PAD - N_EXPERTS)))
    wr_hi = wr.astype(BF16)
    wr_lo = (wr - wr_hi.astype(F32)).astype(BF16)
    br = jnp.pad(b_router, ((0, 0), (0, ROUTER_PAD - N_EXPERTS)), constant_values=-jnp.inf)[:, None, :]
    ltri = (jnp.arange(MOE_TT)[:, None] >= jnp.arange(MOE_TT)[None, :]).astype(BF16)

    xs = x.reshape(n, D_MODEL)
    for l in range(depth):
        xs = _mixer(xs, batch, seq, sinks[l], g_mix[l][None], w_in_b[l], gqk[l], gqm[l], bd_qk, bd_mem,
                    bias, conv_w[l], mk_all[l], mv_all[l], gout[l], w_out_b[l])
        i = l // 2
        gf = g_ffn[l][None]
        if l % 2 == 0:
            xs = _dense_ffn(xs, gf, w_gate_dense[i].astype(BF16), w_up_dense[i].astype(BF16),
                            w_down_dense[i].astype(BF16))
        else:
            hn, route_c, route_t, cnt = _router(xs, gf, wr_hi[i], wr_lo[i], br[i], ltri)
            a, tile_expert, tile_block, n_used = _routing_tables(cnt, n)
            zeros = jnp.zeros((_sorted_rows(n), D_MODEL), BF16)
            x_sorted = _dispatch(hn, route_t, a, zeros)
            y_sorted = _moe_ffn(x_sorted, tile_expert, tile_block, n_used, w_gate_moe[i].astype(BF16),
                                w_up_moe[i].astype(BF16), w_down_moe[i].astype(BF16), zeros)
            xs = _combine(xs, route_c, y_sorted, a)
    return xs.reshape(batch, seq, D_MODEL)
```

```python
import functools

import jax
import jax.numpy as jnp
from jax import lax
from jax.experimental import pallas as pl
from jax.experimental.pallas import tpu as pltpu

F32 = jnp.float32
BF16 = jnp.bfloat16

D_MODEL = 1024
CHUNK = 64
HEAD_DIM = 64
SWA_HEADS = 8
SWA_KV_HEADS = 2
SWA_GROUP = SWA_HEADS // SWA_KV_HEADS
CONV_WIDTH = 256
CONV_K = 3
MEM_HEADS = 4
SWA_WIDTH = SWA_HEADS * HEAD_DIM
KV_WIDTH = SWA_KV_HEADS * HEAD_DIM
MEM_WIDTH = MEM_HEADS * HEAD_DIM
QK_WIDTH = SWA_WIDTH + KV_WIDTH
IN_WIDTH = SWA_WIDTH + 2 * KV_WIDTH + 3 * CONV_WIDTH + MEM_WIDTH
D_FF = 3584
N_EXPERTS = 8
EPS = 1e-6

PAIR = 2 * CHUNK
WIN = 4 * CHUNK
HALO = WIN - PAIR
CONV_HALO = 8
LANES = 128
ROUTER_PAD = LANES

VMEM_LIMIT_BYTES = 56 * 1024 * 1024

NT_DIMS = (((1,), (1,)), ((), ()))


def _rms_scale(x, width):
    return lax.rsqrt(jnp.sum(x * x, axis=-1, keepdims=True) * (1.0 / width) + EPS)


def _head_rms(t, bd_ref):
    ss = jnp.dot((t * t).astype(BF16), bd_ref[...], preferred_element_type=F32)
    return lax.rsqrt(ss * (1.0 / HEAD_DIM) + EPS)


def _low_half(shape):
    return lax.broadcasted_iota(jnp.int32, shape, len(shape) - 1) % LANES < HEAD_DIM


def _softmax_rows(s, sink=None):
    m = jnp.max(s, axis=-1, keepdims=True)
    if sink is not None:
        m = jnp.maximum(m, sink)
    p = jnp.exp(s - m)
    den = jnp.sum(p, axis=-1, keepdims=True)
    if sink is not None:
        den = den + jnp.exp(sink - m)
    return (p * (1.0 / den)).astype(BF16)


def _memkv_kernel(mem_ref, g_ref, w_ref, gk_ref, bd_ref, mka_ref, mkb_ref, mva_ref, mvb_ref):
    m = mem_ref[...]
    hm = (m * _rms_scale(m, D_MODEL) * g_ref[...]).astype(BF16)
    kv = jnp.dot(hm, w_ref[...], preferred_element_type=F32)
    k = kv[:, :MEM_WIDTH]
    k = k * _head_rms(k, bd_ref) * gk_ref[...]
    v = kv[:, MEM_WIDTH:]
    low = _low_half(k.shape)
    mka_ref[...] = jnp.where(low, k, 0.0).astype(BF16)
    mkb_ref[...] = jnp.where(low, 0.0, k).astype(BF16)
    mva_ref[...] = jnp.where(low, v, 0.0).astype(BF16)
    mvb_ref[...] = jnp.where(low, 0.0, v).astype(BF16)


def _mem_kv(mem2d, g_mem, w_mem_kv, gk_mem, bd_mem):
    depth = g_mem.shape[0]
    rows = mem2d.shape[0]
    tr = min(rows, 512)
    out = jax.ShapeDtypeStruct((depth, rows, MEM_WIDTH), BF16)
    out_spec = pl.BlockSpec((None, tr, MEM_WIDTH), lambda l, i: (l, i, 0))
    return pl.pallas_call(
        _memkv_kernel,
        grid=(depth, rows // tr),
        in_specs=[
            pl.BlockSpec((tr, D_MODEL), lambda l, i: (i, 0)),
            pl.BlockSpec((None, 1, D_MODEL), lambda l, i: (l, 0, 0)),
            pl.BlockSpec((None, D_MODEL, 2 * MEM_WIDTH), lambda l, i: (l, 0, 0)),
            pl.BlockSpec((None, 1, MEM_WIDTH), lambda l, i: (l, 0, 0)),
            pl.BlockSpec((MEM_WIDTH, MEM_WIDTH), lambda l, i: (0, 0)),
        ],
        out_specs=[out_spec] * 4,
        out_shape=[out] * 4,
        compiler_params=pltpu.CompilerParams(
            dimension_semantics=("arbitrary", "arbitrary"), vmem_limit_bytes=VMEM_LIMIT_BYTES),
        name="mem_kv",
    )(mem2d, g_mem, w_mem_kv, gk_mem, bd_mem)


MEM_ROWS = 256


def _mixer_kernel(sinks_ref, x_ref, gmix_ref, win_ref, gqk_ref, gqm_ref, bdqk_ref, bdm_ref, bias_ref,
                  convw_ref, mka_ref, mkb_ref, mva_ref, mvb_ref, gout_ref, wout_ref, o_ref,
                  kbuf, vbuf, zbuf, yswa, ymem, *, ts):
    s_idx = pl.program_id(1)
    x = x_ref[...]
    h = (x * _rms_scale(x, D_MODEL) * gmix_ref[...]).astype(BF16)
    proj = jnp.dot(h, win_ref[...], preferred_element_type=F32)

    @pl.when(s_idx == 0)
    def _():
        kbuf[:, 0:HALO, :] = jnp.zeros((2 * SWA_KV_HEADS, HALO, LANES), BF16)
        vbuf[:, 0:HALO, :] = jnp.zeros((2 * SWA_KV_HEADS, HALO, LANES), BF16)
        zbuf[0:CONV_HALO, :] = jnp.zeros((CONV_HALO, CONV_WIDTH), F32)

    qk = proj[:, :QK_WIDTH]
    qk = qk * _head_rms(qk, bdqk_ref) * gqk_ref[...]
    q = qk[:, :SWA_WIDTH].astype(BF16)
    low = _low_half((ts, KV_WIDTH))
    for buf, t in ((kbuf, qk[:, SWA_WIDTH:]), (vbuf, proj[:, QK_WIDTH:QK_WIDTH + KV_WIDTH])):
        swapped = pltpu.roll(t, HEAD_DIM, 1)
        buf[0, HALO:HALO + ts, :] = jnp.where(low, t, 0.0).astype(BF16)
        buf[1, HALO:HALO + ts, :] = jnp.where(low, 0.0, swapped).astype(BF16)
        buf[2, HALO:HALO + ts, :] = jnp.where(low, swapped, 0.0).astype(BF16)
        buf[3, HALO:HALO + ts, :] = jnp.where(low, 0.0, t).astype(BF16)

    key_lane = lax.broadcasted_iota(jnp.int32, (1, WIN), 1)
    first_keys_valid = jnp.logical_or(key_lane >= HALO, s_idx > 0)

    for j in range(ts // PAIR):
        r0 = j * PAIR
        for kh in range(SWA_KV_HEADS):
            c0 = kh * SWA_GROUP * HEAD_DIM
            qg = jnp.concatenate([q[r0:r0 + PAIR, c0:c0 + LANES],
                                  q[r0:r0 + PAIR, c0 + LANES:c0 + 2 * LANES]], axis=0)
            probs = []
            for half in range(2):
                kk = kbuf[2 * kh + half, r0:r0 + WIN, :]
                s_all = lax.dot_general(qg, kk, NT_DIMS, preferred_element_type=F32)
                per_pair = []
                for pr in range(2):
                    hd = kh * SWA_GROUP + 2 * pr + half
                    sg = s_all[pr * PAIR:(pr + 1) * PAIR] + bias_ref[hd]
                    if j == 0:
                        sg = jnp.where(first_keys_valid, sg, -jnp.inf)
                    per_pair.append(_softmax_rows(sg, sinks_ref[hd]))
                probs.append(jnp.concatenate(per_pair, axis=0))
            o = (jnp.dot(probs[0], vbuf[2 * kh, r0:r0 + WIN, :], preferred_element_type=F32)
                 + jnp.dot(probs[1], vbuf[2 * kh + 1, r0:r0 + WIN, :], preferred_element_type=F32))
            yswa[r0:r0 + PAIR, c0:c0 + LANES] = o[0:PAIR]
            yswa[r0:r0 + PAIR, c0 + LANES:c0 + 2 * LANES] = o[PAIR:]

    kbuf[:, 0:HALO, :] = kbuf[:, ts:ts + HALO, :]
    vbuf[:, 0:HALO, :] = vbuf[:, ts:ts + HALO, :]

    c0 = QK_WIDTH + KV_WIDTH
    gate_b = proj[:, c0:c0 + CONV_WIDTH]
    z = proj[:, c0 + CONV_WIDTH:c0 + 2 * CONV_WIDTH] * proj[:, c0 + 2 * CONV_WIDTH:c0 + 3 * CONV_WIDTH]
    zbuf[CONV_HALO:CONV_HALO + ts, :] = z
    z1 = zbuf[CONV_HALO - 1:CONV_HALO - 1 + ts, :]
    z2 = zbuf[CONV_HALO - 2:CONV_HALO - 2 + ts, :]
    cw = convw_ref[...]
    y_conv = gate_b * (cw[0:1] * z2 + cw[1:2] * z1 + cw[2:3] * z)
    zbuf[0:CONV_HALO, :] = zbuf[ts:ts + CONV_HALO, :]

    qm = proj[:, IN_WIDTH - MEM_WIDTH:]
    qm = (qm * _head_rms(qm, bdm_ref) * gqm_ref[...]).astype(BF16)
    mem_rows = min(ts, MEM_ROWS)
    for pr in range(MEM_HEADS // 2):
        cols = slice(pr * LANES, (pr + 1) * LANES)
        for rb in range(ts // mem_rows):
            rows = slice(rb * mem_rows, (rb + 1) * mem_rows)
            qp = qm[rows, cols]
            pa = _softmax_rows(lax.dot_general(qp, mka_ref[:, cols], NT_DIMS, preferred_element_type=F32))
            pb = _softmax_rows(lax.dot_general(qp, mkb_ref[:, cols], NT_DIMS, preferred_element_type=F32))
            ymem[rows, cols] = (jnp.dot(pa, mva_ref[:, cols], preferred_element_type=F32)
                                + jnp.dot(pb, mvb_ref[:, cols], preferred_element_type=F32))

    gout = gout_ref[...]
    ys = yswa[...]
    ym = ymem[...]
    a = (ys * _rms_scale(ys, SWA_WIDTH) * gout[:, :SWA_WIDTH]).astype(BF16)
    b = (y_conv * _rms_scale(y_conv, CONV_WIDTH) * gout[:, SWA_WIDTH:SWA_WIDTH + CONV_WIDTH]).astype(BF16)
    c = (ym * _rms_scale(ym, MEM_WIDTH) * gout[:, SWA_WIDTH + CONV_WIDTH:]).astype(BF16)
    out = x + jnp.dot(a, wout_ref[0:SWA_WIDTH, :], preferred_element_type=F32)
    out = out + jnp.dot(b, wout_ref[SWA_WIDTH:SWA_WIDTH + CONV_WIDTH, :], preferred_element_type=F32)
    out = out + jnp.dot(c, wout_ref[SWA_WIDTH + CONV_WIDTH:, :], preferred_element_type=F32)
    o_ref[...] = out


def _mixer(x2d, batch, seq, sinks, gmix, w_in, gqk, gqm, bd_qk, bd_mem, bias, conv_w, mem_kv, gout, w_out):
    ts = min(seq, 1024)
    nseq = seq // ts
    mem_len = mem_kv[0].shape[0] // batch
    const = lambda b, s, sk: (0, 0)
    mem_spec = pl.BlockSpec((mem_len, MEM_WIDTH), lambda b, s, sk: (b, 0))
    grid_spec = pltpu.PrefetchScalarGridSpec(
        num_scalar_prefetch=1,
        grid=(batch, nseq),
        in_specs=[
            pl.BlockSpec((ts, D_MODEL), lambda b, s, sk: (b * nseq + s, 0)),
            pl.BlockSpec((1, D_MODEL), const),
            pl.BlockSpec((D_MODEL, IN_WIDTH), const),
            pl.BlockSpec((1, QK_WIDTH), const),
            pl.BlockSpec((1, MEM_WIDTH), const),
            pl.BlockSpec((QK_WIDTH, QK_WIDTH), const),
            pl.BlockSpec((MEM_WIDTH, MEM_WIDTH), const),
            pl.BlockSpec((SWA_HEADS, PAIR, WIN), lambda b, s, sk: (0, 0, 0)),
            pl.BlockSpec((CONV_K, CONV_WIDTH), const),
            mem_spec, mem_spec, mem_spec, mem_spec,
            pl.BlockSpec((1, D_MODEL), const),
            pl.BlockSpec((D_MODEL, D_MODEL), const),
        ],
        out_specs=pl.BlockSpec((ts, D_MODEL), lambda b, s, sk: (b * nseq + s, 0)),
        scratch_shapes=[
            pltpu.VMEM((2 * SWA_KV_HEADS, ts + HALO, LANES), BF16),
            pltpu.VMEM((2 * SWA_KV_HEADS, ts + HALO, LANES), BF16),
            pltpu.VMEM((ts + CONV_HALO, CONV_WIDTH), F32),
            pltpu.VMEM((ts, SWA_WIDTH), F32),
            pltpu.VMEM((ts, MEM_WIDTH), F32),
        ],
    )
    return pl.pallas_call(
        functools.partial(_mixer_kernel, ts=ts),
        grid_spec=grid_spec,
        out_shape=jax.ShapeDtypeStruct(x2d.shape, F32),
        compiler_params=pltpu.CompilerParams(
            dimension_semantics=("arbitrary", "arbitrary"), vmem_limit_bytes=VMEM_LIMIT_BYTES),
        name="token_mixer",
    )(sinks, x2d, gmix, w_in, gqk, gqm, bd_qk, bd_mem, bias, conv_w, *mem_kv, gout, w_out)


FF_CHUNK = 256


def _swiglu_accumulate(h, wg_ref, wu_ref, wd_ref, acc_ref):
    tf = wg_ref.shape[-1]
    for c in range(tf // FF_CHUNK):
        sl = slice(c * FF_CHUNK, (c + 1) * FF_CHUNK)
        gate = jnp.dot(h, wg_ref[:, sl], preferred_element_type=F32)
        up = jnp.dot(h, wu_ref[:, sl], preferred_element_type=F32)
        act = (gate * jax.nn.sigmoid(gate) * up).astype(BF16)
        acc_ref[...] += jnp.dot(act, wd_ref[sl, :], preferred_element_type=F32)


def _dense_ffn_kernel(x_ref, g_ref, wg_ref, wu_ref, wd_ref, o_ref, h_ref):
    @pl.when(pl.program_id(1) == 0)
    def _():
        x = x_ref[...]
        h_ref[...] = (x * _rms_scale(x, D_MODEL) * g_ref[...]).astype(BF16)
        o_ref[...] = x

    _swiglu_accumulate(h_ref[...], wg_ref, wu_ref, wd_ref, o_ref)


def _dense_ffn(x2d, g, layer, wg, wu, wd):
    n = x2d.shape[0]
    tm = min(n, 1024)
    tf = D_FF // 2
    return pl.pallas_call(
        _dense_ffn_kernel,
        grid=(n // tm, D_FF // tf),
        in_specs=[
            pl.BlockSpec((tm, D_MODEL), lambda i, f: (i, 0)),
            pl.BlockSpec((1, D_MODEL), lambda i, f: (0, 0)),
            pl.BlockSpec((None, D_MODEL, tf), lambda i, f: (layer, 0, f)),
            pl.BlockSpec((None, D_MODEL, tf), lambda i, f: (layer, 0, f)),
            pl.BlockSpec((None, tf, D_MODEL), lambda i, f: (layer, f, 0)),
        ],
        out_specs=pl.BlockSpec((tm, D_MODEL), lambda i, f: (i, 0)),
        out_shape=jax.ShapeDtypeStruct(x2d.shape, F32),
        scratch_shapes=[pltpu.VMEM((tm, D_MODEL), BF16)],
        compiler_params=pltpu.CompilerParams(
            dimension_semantics=("arbitrary", "arbitrary"), vmem_limit_bytes=VMEM_LIMIT_BYTES),
        name="dense_ffn",
    )(x2d, g, wg, wu, wd)


MOE_TT = 256
MOE_TM = 512
ROW_ALIGN = 16
MOE_WIN = MOE_TT + ROW_ALIGN
DISPATCH_FAST_WIN = 144
DISPATCH_FAST_FILL = DISPATCH_FAST_WIN - ROW_ALIGN
COMBINE_FAST_WIN = 256
REGION_SLACK = MOE_WIN
ROUTE_ROWS = 8


def _moe_ffn_kernel(te_ref, tb_ref, nu_ref, x_ref, wg_ref, wu_ref, wd_ref, ys_hbm, o_ref, acc_ref):
    del te_ref, tb_ref, ys_hbm
    f = pl.program_id(1)
    used = pl.program_id(0) < nu_ref[0]

    @pl.when(jnp.logical_and(used, f == 0))
    def _():
        acc_ref[...] = jnp.zeros(acc_ref.shape, F32)

    @pl.when(used)
    def _():
        _swiglu_accumulate(x_ref[...], wg_ref, wu_ref, wd_ref, acc_ref)

    @pl.when(jnp.logical_and(used, f == pl.num_programs(1) - 1))
    def _():
        o_ref[...] = acc_ref[...].astype(BF16)


def _moe_ffn(xs, tile_expert, tile_block, n_used, layer, wg, wu, wd, ys):
    p = xs.shape[0]
    tm = MOE_TM
    tf = D_FF // 2
    nf = D_FF // tf
    max_tiles = tile_expert.shape[0]

    def f_eff(i, f, nu):
        return jnp.where(i < nu[0], f, nf - 1)

    grid_spec = pltpu.PrefetchScalarGridSpec(
        num_scalar_prefetch=3,
        grid=(max_tiles, nf),
        in_specs=[
            pl.BlockSpec((tm, D_MODEL), lambda i, f, te, tb, nu: (tb[i], 0)),
            pl.BlockSpec((None, None, D_MODEL, tf), lambda i, f, te, tb, nu: (layer, te[i], 0, f_eff(i, f, nu))),
            pl.BlockSpec((None, None, D_MODEL, tf), lambda i, f, te, tb, nu: (layer, te[i], 0, f_eff(i, f, nu))),
            pl.BlockSpec((None, None, tf, D_MODEL), lambda i, f, te, tb, nu: (layer, te[i], f_eff(i, f, nu), 0)),
            pl.BlockSpec(memory_space=pl.ANY),
        ],
        out_specs=pl.BlockSpec((tm, D_MODEL), lambda i, f, te, tb, nu: (tb[i], 0)),
        scratch_shapes=[pltpu.VMEM((tm, D_MODEL), F32)],
    )
    return pl.pallas_call(
        _moe_ffn_kernel,
        grid_spec=grid_spec,
        out_shape=jax.ShapeDtypeStruct((p, D_MODEL), BF16),
        input_output_aliases={7: 0},
        compiler_params=pltpu.CompilerParams(
            dimension_semantics=("arbitrary", "arbitrary"), vmem_limit_bytes=VMEM_LIMIT_BYTES),
        name="moe_ffn",
    )(tile_expert, tile_block, n_used, xs, wg, wu, wd, ys)


def _router_kernel(x_ref, g_ref, wh_ref, wl_ref, b_ref, ltri_ref, hn_ref, rc_ref, rt_ref, cnt_ref):
    x = x_ref[...]
    h = x * _rms_scale(x, D_MODEL) * g_ref[...]
    hh = h.astype(BF16)
    hn_ref[...] = hh
    hl = (h - hh.astype(F32)).astype(BF16)
    wh = wh_ref[...]
    logits = (jnp.dot(hh, wh, preferred_element_type=F32)
              + jnp.dot(hl, wh, preferred_element_type=F32)
              + jnp.dot(hh, wl_ref[...], preferred_element_type=F32)) + b_ref[...]
    lane = lax.broadcasted_iota(jnp.int32, logits.shape, 1)
    v1 = jnp.max(logits, axis=-1, keepdims=True)
    i1 = jnp.min(jnp.where(logits == v1, lane, ROUTER_PAD), axis=-1, keepdims=True)
    rest = jnp.where(lane == i1, -jnp.inf, logits)
    v2 = jnp.max(rest, axis=-1, keepdims=True)
    i2 = jnp.min(jnp.where(rest == v2, lane, ROUTER_PAD), axis=-1, keepdims=True)
    e2 = jnp.exp(v2 - v1)
    den = 1.0 + e2
    chosen = jnp.logical_or(lane == i1, lane == i2)
    cum = jnp.dot(ltri_ref[...], jnp.where(chosen, 1.0, 0.0).astype(BF16), preferred_element_type=F32)
    r1 = jnp.sum(jnp.where(lane == i1, cum, 0.0), axis=-1, keepdims=True) - 1.0
    r2 = jnp.sum(jnp.where(lane == i2, cum, 0.0), axis=-1, keepdims=True) - 1.0
    rc = jnp.zeros(logits.shape, F32)
    for k, col in enumerate((i1.astype(F32), i2.astype(F32), r1, r2, 1.0 / den, e2 / den)):
        rc = jnp.where(lane == k, col, rc)
    rc_ref[...] = rc
    rt_ref[...] = rc.T[:ROUTE_ROWS, :]
    cnt_ref[...] = cum[MOE_TT - 1:MOE_TT, :].astype(jnp.int32)


def _router(x2d, g, w_hi, w_lo, b_pad, ltri):
    n = x2d.shape[0]
    tt = MOE_TT
    nt = n // tt
    return pl.pallas_call(
        _router_kernel,
        grid=(nt,),
        in_specs=[
            pl.BlockSpec((tt, D_MODEL), lambda i: (i, 0)),
            pl.BlockSpec((1, D_MODEL), lambda i: (0, 0)),
            pl.BlockSpec((D_MODEL, ROUTER_PAD), lambda i: (0, 0)),
            pl.BlockSpec((D_MODEL, ROUTER_PAD), lambda i: (0, 0)),
            pl.BlockSpec((1, ROUTER_PAD), lambda i: (0, 0)),
            pl.BlockSpec((tt, tt), lambda i: (0, 0)),
        ],
        out_specs=[
            pl.BlockSpec((tt, D_MODEL), lambda i: (i, 0)),
            pl.BlockSpec((tt, ROUTER_PAD), lambda i: (i, 0)),
            pl.BlockSpec((ROUTE_ROWS, tt), lambda i: (0, i)),
            pl.BlockSpec((None, 1, ROUTER_PAD), lambda i: (i, 0, 0)),
        ],
        out_shape=[jax.ShapeDtypeStruct((n, D_MODEL), BF16),
                   jax.ShapeDtypeStruct((n, ROUTER_PAD), F32),
                   jax.ShapeDtypeStruct((ROUTE_ROWS, n), F32),
                   jax.ShapeDtypeStruct((nt, 1, ROUTER_PAD), jnp.int32)],
        compiler_params=pltpu.CompilerParams(
            dimension_semantics=("arbitrary",), vmem_limit_bytes=VMEM_LIMIT_BYTES),
        name="router",
    )(x2d, g, w_hi, w_lo, b_pad, ltri)


def _routing_tables(cnt, n):
    i32 = jnp.int32
    nt = n // MOE_TT
    cnt = cnt.reshape(nt, ROUTER_PAD)[:, :N_EXPERTS]
    total = jnp.sum(cnt, axis=0)
    region = ((total + REGION_SLACK + MOE_TM - 1) // MOE_TM) * MOE_TM
    start = jnp.cumsum(region) - region
    first = start[None, :] + jnp.cumsum(cnt, axis=0) - cnt
    a = jnp.concatenate([first, (start + total)[None, :]], axis=0).reshape(-1).astype(i32)
    ntile = (total + MOE_TM - 1) // MOE_TM
    tend = jnp.cumsum(ntile)
    n_used = tend[-1]
    max_tiles = 2 * n // MOE_TM + N_EXPERTS
    tile = jnp.minimum(jnp.arange(max_tiles, dtype=i32), n_used - 1)
    tile_expert = jnp.minimum(jnp.sum((tend[None, :] <= tile[:, None]).astype(i32), axis=1), N_EXPERTS - 1)
    tile_block = start[tile_expert] // MOE_TM + tile - (tend - ntile)[tile_expert]
    fill = jnp.max(jnp.bitwise_and(first, ROW_ALIGN - 1) + cnt, axis=1)
    fast_dispatch = jnp.logical_and(fill <= DISPATCH_FAST_FILL, jnp.arange(nt) < nt - 1).astype(i32)
    fast_combine = (fill <= COMBINE_FAST_WIN).astype(i32)
    return (a, tile_expert.astype(i32), tile_block.astype(i32), n_used.reshape(1).astype(i32),
            fast_dispatch, fast_combine)


def _sorted_rows(n):
    return -(-(2 * n + N_EXPERTS * (REGION_SLACK + MOE_TM)) // MOE_TM) * MOE_TM


def _window(a_ref, t, e):
    a = a_ref[t * N_EXPERTS + e]
    off = jnp.bitwise_and(a, ROW_ALIGN - 1)
    return pl.multiple_of(a - off, ROW_ALIGN), off


def _expert_slot(e, e1, e2, r1, r2, off):
    d = jnp.where(e1 == e, r1, jnp.where(e2 == e, r2, -1.0))
    return jnp.where(d >= 0.0, d + off.astype(F32), -1.0).astype(jnp.int32)


def _dispatch_kernel(a_ref, fast_ref, hn_ref, rt_ref, zeros_hbm, xs_hbm, carry, stage, sem):
    del zeros_hbm
    t = pl.program_id(0)
    nt = pl.num_programs(0)
    slot = t % 2

    def window_copy(sl, e, base, win):
        return pltpu.make_async_copy(stage.at[sl, e, 0:win], xs_hbm.at[pl.ds(base, win)], sem)

    @pl.when(t == 0)
    def _():
        carry[...] = jnp.zeros(carry.shape, F32)

    def stage_and_send(win, prev_win_is_fast):
        hn = hn_ref[...]
        rt = rt_ref[...]
        e1, e2, r1, r2 = rt[0:1], rt[1:2], rt[2:3], rt[3:4]
        row = lax.broadcasted_iota(jnp.int32, (win, MOE_TT), 0)
        head_row = lax.broadcasted_iota(jnp.int32, (ROW_ALIGN, 1), 0)
        bases = []
        for e in range(N_EXPERTS):
            base, off = _window(a_ref, t, e)
            bases.append(base)
            slot_of = _expert_slot(e, e1, e2, r1, r2, off)
            sel = jnp.where(row == slot_of, 1.0, 0.0).astype(BF16)
            rows = jnp.dot(sel, hn, preferred_element_type=F32)
            head = jnp.where(head_row < off, carry[e], rows[0:ROW_ALIGN])
            stage[slot, e, 0:ROW_ALIGN, :] = head.astype(BF16)
            stage[slot, e, ROW_ALIGN:win, :] = rows[ROW_ALIGN:].astype(BF16)
            filled = off + a_ref[(t + 1) * N_EXPERTS + e] - a_ref[t * N_EXPERTS + e]
            last_group = pl.multiple_of(lax.shift_right_logical(filled, 4) * ROW_ALIGN, ROW_ALIGN)
            carry[e] = stage[slot, e, pl.ds(last_group, ROW_ALIGN), :].astype(F32)

        for was_fast, prev_win in ((1, DISPATCH_FAST_WIN), (0, MOE_WIN)):
            @pl.when(jnp.logical_and(t > 0, prev_win_is_fast == was_fast))
            def _():
                for e in range(N_EXPERTS):
                    window_copy(1 - slot, e, 0, prev_win).wait()

        for e in range(N_EXPERTS):
            window_copy(slot, e, bases[e], win).start()

    is_fast = fast_ref[t]
    prev_fast = fast_ref[jnp.maximum(t - 1, 0)]

    @pl.when(is_fast == 1)
    def _():
        stage_and_send(DISPATCH_FAST_WIN, prev_fast)

    @pl.when(is_fast == 0)
    def _():
        stage_and_send(MOE_WIN, prev_fast)

    @pl.when(t == nt - 1)
    def _():
        for e in range(N_EXPERTS):
            window_copy(slot, e, 0, MOE_WIN).wait()


def _dispatch(hn, route_t, a, fast, zeros):
    n = hn.shape[0]
    tt = MOE_TT
    grid_spec = pltpu.PrefetchScalarGridSpec(
        num_scalar_prefetch=2,
        grid=(n // tt,),
        in_specs=[
            pl.BlockSpec((tt, D_MODEL), lambda t, a, fs: (t, 0)),
            pl.BlockSpec((ROUTE_ROWS, tt), lambda t, a, fs: (0, t)),
            pl.BlockSpec(memory_space=pl.ANY),
        ],
        out_specs=pl.BlockSpec(memory_space=pl.ANY),
        scratch_shapes=[
            pltpu.VMEM((N_EXPERTS, ROW_ALIGN, D_MODEL), F32),
            pltpu.VMEM((2, N_EXPERTS, MOE_WIN, D_MODEL), BF16),
            pltpu.SemaphoreType.DMA(()),
        ],
    )
    return pl.pallas_call(
        _dispatch_kernel,
        grid_spec=grid_spec,
        out_shape=jax.ShapeDtypeStruct(zeros.shape, BF16),
        input_output_aliases={4: 0},
        compiler_params=pltpu.CompilerParams(
            dimension_semantics=("arbitrary",), vmem_limit_bytes=VMEM_LIMIT_BYTES),
        name="moe_dispatch",
    )(a, fast, hn, route_t, zeros)


def _combine_kernel(a_ref, fast_ref, x_ref, rc_ref, ys_hbm, o_ref, ybuf, sem):
    t = pl.program_id(0)
    nt = pl.num_programs(0)
    slot = t % 2

    def window_copy(tile, sl, e):
        base, _ = _window(a_ref, tile, e)
        return pltpu.make_async_copy(ys_hbm.at[pl.ds(base, MOE_WIN)], ybuf.at[sl, e], sem.at[sl])

    @pl.when(t == 0)
    def _():
        for e in range(N_EXPERTS):
            window_copy(0, 0, e).start()

    @pl.when(t + 1 < nt)
    def _():
        for e in range(N_EXPERTS):
            window_copy(t + 1, 1 - slot, e).start()

    for e in range(N_EXPERTS):
        window_copy(t, slot, e).wait()

    rc = rc_ref[...]
    e1, e2, r1, r2, g1, g2 = (rc[:, k:k + 1] for k in range(6))

    def gather_add(win):
        lane = lax.broadcasted_iota(jnp.int32, (MOE_TT, win), 1)
        acc = jnp.zeros((MOE_TT, D_MODEL), F32)
        for e in range(N_EXPERTS):
            _, off = _window(a_ref, t, e)
            slot_of = _expert_slot(e, e1, e2, r1, r2, off)
            sel = jnp.where(lane == slot_of, 1.0, 0.0).astype(BF16)
            gate = jnp.where(e1 == e, g1, jnp.where(e2 == e, g2, 0.0))
            acc = acc + gate * jnp.dot(sel, ybuf[slot, e, 0:win, :], preferred_element_type=F32)
        o_ref[...] = x_ref[...] + acc

    @pl.when(fast_ref[t] == 1)
    def _():
        gather_add(COMBINE_FAST_WIN)

    @pl.when(fast_ref[t] == 0)
    def _():
        gather_add(MOE_WIN)


def _combine(x2d, route_c, ys, a, fast):
    n = x2d.shape[0]
    tt = MOE_TT
    grid_spec = pltpu.PrefetchScalarGridSpec(
        num_scalar_prefetch=2,
        grid=(n // tt,),
        in_specs=[
            pl.BlockSpec((tt, D_MODEL), lambda t, a, fs: (t, 0)),
            pl.BlockSpec((tt, ROUTER_PAD), lambda t, a, fs: (t, 0)),
            pl.BlockSpec(memory_space=pl.ANY),
        ],
        out_specs=pl.BlockSpec((tt, D_MODEL), lambda t, a, fs: (t, 0)),
        scratch_shapes=[
            pltpu.VMEM((2, N_EXPERTS, MOE_WIN, D_MODEL), BF16),
            pltpu.SemaphoreType.DMA((2,)),
        ],
    )
    return pl.pallas_call(
        _combine_kernel,
        grid_spec=grid_spec,
        out_shape=jax.ShapeDtypeStruct((n, D_MODEL), F32),
        compiler_params=pltpu.CompilerParams(
            dimension_semantics=("arbitrary",), vmem_limit_bytes=VMEM_LIMIT_BYTES),
        name="moe_combine",
    )(a, fast, x2d, route_c, ys)


def _block_diag_ones(width):
    r = jnp.arange(width) // HEAD_DIM
    return (r[:, None] == r[None, :]).astype(BF16)


def _swa_bias():
    qi = jnp.arange(PAIR)[:, None]
    kj = jnp.arange(WIN)[None, :]
    dist = jnp.abs(qi + HALO - kj).astype(F32)
    kc = kj // CHUNK
    qc = qi // CHUNK
    visible = jnp.logical_and(kc >= qc, kc <= qc + 2)
    slopes = jnp.asarray([2.0 ** (-8.0 * (i + 1) / SWA_HEADS) for i in range(SWA_HEADS)], F32)
    bias = -slopes[:, None, None] * dist[None]
    return jnp.where(visible[None], bias, -jnp.inf)


def kernel(x, mem, g_mix, w_in, g_q_swa, g_k_swa, sinks, conv_w, g_mem, w_mem_kv, g_q_mem, g_k_mem,
           g_out_swa, g_out_conv, g_out_mem, w_out, g_ffn, w_gate_dense, w_up_dense, w_down_dense,
           w_router, b_router, w_gate_moe, w_up_moe, w_down_moe):
    batch, seq, _ = x.shape
    depth = g_mix.shape[0]
    n = batch * seq
    scale = HEAD_DIM ** -0.5

    bd_qk = _block_diag_ones(QK_WIDTH)
    bd_mem = _block_diag_ones(MEM_WIDTH)
    bias = _swa_bias()
    gqk = jnp.concatenate([jnp.tile(g_q_swa * scale, (1, SWA_HEADS)),
                           jnp.tile(g_k_swa, (1, SWA_KV_HEADS))], axis=1)[:, None, :]
    gqm = jnp.tile(g_q_mem * scale, (1, MEM_HEADS))[:, None, :]
    gkm = jnp.tile(g_k_mem, (1, MEM_HEADS))[:, None, :]
    gout = jnp.concatenate([g_out_swa, g_out_conv, g_out_mem], axis=1)[:, None, :]

    w_in_b = w_in.astype(BF16)
    w_out_b = w_out.astype(BF16)
    mem_kv_all = _mem_kv(mem.reshape(-1, D_MODEL), g_mem[:, None, :], w_mem_kv.astype(BF16), gkm, bd_mem)

    wr = jnp.pad(w_router, ((0, 0), (0, 0), (0, ROUTER_PAD - N_EXPERTS)))
    wr_hi = wr.astype(BF16)
    wr_lo = (wr - wr_hi.astype(F32)).astype(BF16)
    br = jnp.pad(b_router, ((0, 0), (0, ROUTER_PAD - N_EXPERTS)), constant_values=-jnp.inf)[:, None, :]
    ltri = (jnp.arange(MOE_TT)[:, None] >= jnp.arange(MOE_TT)[None, :]).astype(BF16)

    wg_dense, wu_dense, wd_dense = (w.astype(BF16) for w in (w_gate_dense, w_up_dense, w_down_dense))
    wg_moe, wu_moe, wd_moe = (w.astype(BF16) for w in (w_gate_moe, w_up_moe, w_down_moe))

    xs = x.reshape(n, D_MODEL)
    for l in range(depth):
        xs = _mixer(xs, batch, seq, sinks[l], g_mix[l][None], w_in_b[l], gqk[l], gqm[l], bd_qk, bd_mem,
                    bias, conv_w[l], [t[l] for t in mem_kv_all], gout[l], w_out_b[l])
        i = l // 2
        gf = g_ffn[l][None]
        if l % 2 == 0:
            xs = _dense_ffn(xs, gf, i, wg_dense, wu_dense, wd_dense)
        else:
            hn, route_c, route_t, cnt = _router(xs, gf, wr_hi[i], wr_lo[i], br[i], ltri)
            a, tile_expert, tile_block, n_used, fast_d, fast_c = _routing_tables(cnt, n)
            sorted_shape = (_sorted_rows(n), D_MODEL)
            x_sorted = _dispatch(hn, route_t, a, fast_d, jnp.zeros(sorted_shape, BF16))
            y_sorted = _moe_ffn(x_sorted, tile_expert, tile_block, n_used, i, wg_moe, wu_moe, wd_moe,
                                jnp.zeros(sorted_shape, BF16))
            xs = _combine(xs, route_c, y_sorted, a, fast_c)
    return xs.reshape(batch, seq, D_MODEL)
```

```python
import functools

import jax
import jax.numpy as jnp
from jax import lax
from jax.experimental import pallas as pl
from jax.experimental.pallas import tpu as pltpu

F32 = jnp.float32
BF16 = jnp.bfloat16

D_MODEL = 1024
CHUNK = 64
HEAD_DIM = 64
SWA_HEADS = 8
SWA_KV_HEADS = 2
SWA_GROUP = SWA_HEADS // SWA_KV_HEADS
CONV_WIDTH = 256
CONV_K = 3
MEM_HEADS = 4
SWA_WIDTH = SWA_HEADS * HEAD_DIM
KV_WIDTH = SWA_KV_HEADS * HEAD_DIM
MEM_WIDTH = MEM_HEADS * HEAD_DIM
QK_WIDTH = SWA_WIDTH + KV_WIDTH
IN_WIDTH = SWA_WIDTH + 2 * KV_WIDTH + 3 * CONV_WIDTH + MEM_WIDTH
D_FF = 3584
N_EXPERTS = 8
EPS = 1e-6

PAIR = 2 * CHUNK
WIN = 4 * CHUNK
HALO = WIN - PAIR
CONV_HALO = 8
LANES = 128
ROUTER_PAD = LANES

VMEM_LIMIT_BYTES = 56 * 1024 * 1024

NT_DIMS = (((1,), (1,)), ((), ()))


def _rms_scale(x, width):
    return lax.rsqrt(jnp.sum(x * x, axis=-1, keepdims=True) * (1.0 / width) + EPS)


def _head_rms(t, bd_ref):
    ss = jnp.dot((t * t).astype(BF16), bd_ref[...], preferred_element_type=F32)
    return lax.rsqrt(ss * (1.0 / HEAD_DIM) + EPS)


def _low_half(shape):
    return lax.broadcasted_iota(jnp.int32, shape, len(shape) - 1) % LANES < HEAD_DIM


def _softmax_rows(s, sink=None):
    m = jnp.max(s, axis=-1, keepdims=True)
    if sink is not None:
        m = jnp.maximum(m, sink)
    p = jnp.exp(s - m)
    den = jnp.sum(p, axis=-1, keepdims=True)
    if sink is not None:
        den = den + jnp.exp(sink - m)
    return (p * (1.0 / den)).astype(BF16)


def _memkv_kernel(mem_ref, g_ref, w_ref, gk_ref, bd_ref, mka_ref, mkb_ref, mva_ref, mvb_ref):
    m = mem_ref[...]
    hm = (m * _rms_scale(m, D_MODEL) * g_ref[...]).astype(BF16)
    kv = jnp.dot(hm, w_ref[...], preferred_element_type=F32)
    k = kv[:, :MEM_WIDTH]
    k = k * _head_rms(k, bd_ref) * gk_ref[...]
    v = kv[:, MEM_WIDTH:]
    low = _low_half(k.shape)
    mka_ref[...] = jnp.where(low, k, 0.0).astype(BF16)
    mkb_ref[...] = jnp.where(low, 0.0, k).astype(BF16)
    mva_ref[...] = jnp.where(low, v, 0.0).astype(BF16)
    mvb_ref[...] = jnp.where(low, 0.0, v).astype(BF16)


def _mem_kv(mem2d, g_mem, w_mem_kv, gk_mem, bd_mem):
    depth = g_mem.shape[0]
    rows = mem2d.shape[0]
    tr = min(rows, 512)
    out = jax.ShapeDtypeStruct((depth, rows, MEM_WIDTH), BF16)
    out_spec = pl.BlockSpec((None, tr, MEM_WIDTH), lambda l, i: (l, i, 0))
    return pl.pallas_call(
        _memkv_kernel,
        grid=(depth, rows // tr),
        in_specs=[
            pl.BlockSpec((tr, D_MODEL), lambda l, i: (i, 0)),
            pl.BlockSpec((None, 1, D_MODEL), lambda l, i: (l, 0, 0)),
            pl.BlockSpec((None, D_MODEL, 2 * MEM_WIDTH), lambda l, i: (l, 0, 0)),
            pl.BlockSpec((None, 1, MEM_WIDTH), lambda l, i: (l, 0, 0)),
            pl.BlockSpec((MEM_WIDTH, MEM_WIDTH), lambda l, i: (0, 0)),
        ],
        out_specs=[out_spec] * 4,
        out_shape=[out] * 4,
        compiler_params=pltpu.CompilerParams(
            dimension_semantics=("arbitrary", "arbitrary"), vmem_limit_bytes=VMEM_LIMIT_BYTES),
        name="mem_kv",
    )(mem2d, g_mem, w_mem_kv, gk_mem, bd_mem)


MEM_ROWS = 256


def _mixer_kernel(sinks_ref, x_ref, gmix_ref, win_ref, gqk_ref, gqm_ref, bdqk_ref, bdm_ref, bias_ref,
                  convw_ref, mka_ref, mkb_ref, mva_ref, mvb_ref, gout_ref, wout_ref, o_ref,
                  kbuf, vbuf, zbuf, yswa, ymem, *, ts):
    s_idx = pl.program_id(1)
    x = x_ref[...]
    h = (x * _rms_scale(x, D_MODEL) * gmix_ref[...]).astype(BF16)
    proj = jnp.dot(h, win_ref[...], preferred_element_type=F32)

    @pl.when(s_idx == 0)
    def _():
        kbuf[:, 0:HALO, :] = jnp.zeros((2 * SWA_KV_HEADS, HALO, LANES), BF16)
        vbuf[:, 0:HALO, :] = jnp.zeros((2 * SWA_KV_HEADS, HALO, LANES), BF16)
        zbuf[0:CONV_HALO, :] = jnp.zeros((CONV_HALO, CONV_WIDTH), F32)

    qk = proj[:, :QK_WIDTH]
    qk = qk * _head_rms(qk, bdqk_ref) * gqk_ref[...]
    q = qk[:, :SWA_WIDTH].astype(BF16)
    low = _low_half((ts, KV_WIDTH))
    for buf, t in ((kbuf, qk[:, SWA_WIDTH:]), (vbuf, proj[:, QK_WIDTH:QK_WIDTH + KV_WIDTH])):
        swapped = pltpu.roll(t, HEAD_DIM, 1)
        buf[0, HALO:HALO + ts, :] = jnp.where(low, t, 0.0).astype(BF16)
        buf[1, HALO:HALO + ts, :] = jnp.where(low, 0.0, swapped).astype(BF16)
        buf[2, HALO:HALO + ts, :] = jnp.where(low, swapped, 0.0).astype(BF16)
        buf[3, HALO:HALO + ts, :] = jnp.where(low, 0.0, t).astype(BF16)

    key_lane = lax.broadcasted_iota(jnp.int32, (1, WIN), 1)
    first_keys_valid = jnp.logical_or(key_lane >= HALO, s_idx > 0)

    for j in range(ts // PAIR):
        r0 = j * PAIR
        for kh in range(SWA_KV_HEADS):
            c0 = kh * SWA_GROUP * HEAD_DIM
            qg = jnp.concatenate([q[r0:r0 + PAIR, c0:c0 + LANES],
                                  q[r0:r0 + PAIR, c0 + LANES:c0 + 2 * LANES]], axis=0)
            probs = []
            for half in range(2):
                kk = kbuf[2 * kh + half, r0:r0 + WIN, :]
                s_all = lax.dot_general(qg, kk, NT_DIMS, preferred_element_type=F32)
                per_pair = []
                for pr in range(2):
                    hd = kh * SWA_GROUP + 2 * pr + half
                    sg = s_all[pr * PAIR:(pr + 1) * PAIR] + bias_ref[hd]
                    if j == 0:
                        sg = jnp.where(first_keys_valid, sg, -jnp.inf)
                    per_pair.append(_softmax_rows(sg, sinks_ref[hd]))
                probs.append(jnp.concatenate(per_pair, axis=0))
            o = (jnp.dot(probs[0], vbuf[2 * kh, r0:r0 + WIN, :], preferred_element_type=F32)
                 + jnp.dot(probs[1], vbuf[2 * kh + 1, r0:r0 + WIN, :], preferred_element_type=F32))
            yswa[r0:r0 + PAIR, c0:c0 + LANES] = o[0:PAIR]
            yswa[r0:r0 + PAIR, c0 + LANES:c0 + 2 * LANES] = o[PAIR:]

    kbuf[:, 0:HALO, :] = kbuf[:, ts:ts + HALO, :]
    vbuf[:, 0:HALO, :] = vbuf[:, ts:ts + HALO, :]

    c0 = QK_WIDTH + KV_WIDTH
    gate_b = proj[:, c0:c0 + CONV_WIDTH]
    z = proj[:, c0 + CONV_WIDTH:c0 + 2 * CONV_WIDTH] * proj[:, c0 + 2 * CONV_WIDTH:c0 + 3 * CONV_WIDTH]
    zbuf[CONV_HALO:CONV_HALO + ts, :] = z
    z1 = zbuf[CONV_HALO - 1:CONV_HALO - 1 + ts, :]
    z2 = zbuf[CONV_HALO - 2:CONV_HALO - 2 + ts, :]
    cw = convw_ref[...]
    y_conv = gate_b * (cw[0:1] * z2 + cw[1:2] * z1 + cw[2:3] * z)
    zbuf[0:CONV_HALO, :] = zbuf[ts:ts + CONV_HALO, :]

    qm = proj[:, IN_WIDTH - MEM_WIDTH:]
    qm = (qm * _head_rms(qm, bdm_ref) * gqm_ref[...]).astype(BF16)
    mem_rows = min(ts, MEM_ROWS)
    for pr in range(MEM_HEADS // 2):
        cols = slice(pr * LANES, (pr + 1) * LANES)
        for rb in range(ts // mem_rows):
            rows = slice(rb * mem_rows, (rb + 1) * mem_rows)
            qp = qm[rows, cols]
            pa = _softmax_rows(lax.dot_general(qp, mka_ref[:, cols], NT_DIMS, preferred_element_type=F32))
            pb = _softmax_rows(lax.dot_general(qp, mkb_ref[:, cols], NT_DIMS, preferred_element_type=F32))
            ymem[rows, cols] = (jnp.dot(pa, mva_ref[:, cols], preferred_element_type=F32)
                                + jnp.dot(pb, mvb_ref[:, cols], preferred_element_type=F32))

    gout = gout_ref[...]
    ys = yswa[...]
    ym = ymem[...]
    a = (ys * _rms_scale(ys, SWA_WIDTH) * gout[:, :SWA_WIDTH]).astype(BF16)
    b = (y_conv * _rms_scale(y_conv, CONV_WIDTH) * gout[:, SWA_WIDTH:SWA_WIDTH + CONV_WIDTH]).astype(BF16)
    c = (ym * _rms_scale(ym, MEM_WIDTH) * gout[:, SWA_WIDTH + CONV_WIDTH:]).astype(BF16)
    out = x + jnp.dot(a, wout_ref[0:SWA_WIDTH, :], preferred_element_type=F32)
    out = out + jnp.dot(b, wout_ref[SWA_WIDTH:SWA_WIDTH + CONV_WIDTH, :], preferred_element_type=F32)
    out = out + jnp.dot(c, wout_ref[SWA_WIDTH + CONV_WIDTH:, :], preferred_element_type=F32)
    o_ref[...] = out


def _mixer(x2d, batch, seq, sinks, gmix, w_in, gqk, gqm, bd_qk, bd_mem, bias, conv_w, mem_kv, gout, w_out):
    ts = min(seq, 1024)
    nseq = seq // ts
    mem_len = mem_kv[0].shape[0] // batch
    const = lambda b, s, sk: (0, 0)
    mem_spec = pl.BlockSpec((mem_len, MEM_WIDTH), lambda b, s, sk: (b, 0))
    grid_spec = pltpu.PrefetchScalarGridSpec(
        num_scalar_prefetch=1,
        grid=(batch, nseq),
        in_specs=[
            pl.BlockSpec((ts, D_MODEL), lambda b, s, sk: (b * nseq + s, 0)),
            pl.BlockSpec((1, D_MODEL), const),
            pl.BlockSpec((D_MODEL, IN_WIDTH), const),
            pl.BlockSpec((1, QK_WIDTH), const),
            pl.BlockSpec((1, MEM_WIDTH), const),
            pl.BlockSpec((QK_WIDTH, QK_WIDTH), const),
            pl.BlockSpec((MEM_WIDTH, MEM_WIDTH), const),
            pl.BlockSpec((SWA_HEADS, PAIR, WIN), lambda b, s, sk: (0, 0, 0)),
            pl.BlockSpec((CONV_K, CONV_WIDTH), const),
            mem_spec, mem_spec, mem_spec, mem_spec,
            pl.BlockSpec((1, D_MODEL), const),
            pl.BlockSpec((D_MODEL, D_MODEL), const),
        ],
        out_specs=pl.BlockSpec((ts, D_MODEL), lambda b, s, sk: (b * nseq + s, 0)),
        scratch_shapes=[
            pltpu.VMEM((2 * SWA_KV_HEADS, ts + HALO, LANES), BF16),
            pltpu.VMEM((2 * SWA_KV_HEADS, ts + HALO, LANES), BF16),
            pltpu.VMEM((ts + CONV_HALO, CONV_WIDTH), F32),
            pltpu.VMEM((ts, SWA_WIDTH), F32),
            pltpu.VMEM((ts, MEM_WIDTH), F32),
        ],
    )
    return pl.pallas_call(
        functools.partial(_mixer_kernel, ts=ts),
        grid_spec=grid_spec,
        out_shape=jax.ShapeDtypeStruct(x2d.shape, F32),
        compiler_params=pltpu.CompilerParams(
            dimension_semantics=("arbitrary", "arbitrary"), vmem_limit_bytes=VMEM_LIMIT_BYTES),
        name="token_mixer",
    )(sinks, x2d, gmix, w_in, gqk, gqm, bd_qk, bd_mem, bias, conv_w, *mem_kv, gout, w_out)


FF_CHUNK = 256


def _swiglu_accumulate(h, wg_ref, wu_ref, wd_ref, acc_ref):
    tf = wg_ref.shape[-1]
    for c in range(tf // FF_CHUNK):
        sl = slice(c * FF_CHUNK, (c + 1) * FF_CHUNK)
        gate = jnp.dot(h, wg_ref[:, sl], preferred_element_type=F32)
        up = jnp.dot(h, wu_ref[:, sl], preferred_element_type=F32)
        act = (gate * jax.nn.sigmoid(gate) * up).astype(BF16)
        acc_ref[...] += jnp.dot(act, wd_ref[sl, :], preferred_element_type=F32)


def _dense_ffn_kernel(x_ref, g_ref, wg_ref, wu_ref, wd_ref, o_ref, h_ref):
    @pl.when(pl.program_id(1) == 0)
    def _():
        x = x_ref[...]
        h_ref[...] = (x * _rms_scale(x, D_MODEL) * g_ref[...]).astype(BF16)
        o_ref[...] = x

    _swiglu_accumulate(h_ref[...], wg_ref, wu_ref, wd_ref, o_ref)


def _dense_ffn(x2d, g, layer, wg, wu, wd):
    n = x2d.shape[0]
    tm = min(n, 1024)
    tf = D_FF // 2
    return pl.pallas_call(
        _dense_ffn_kernel,
        grid=(n // tm, D_FF // tf),
        in_specs=[
            pl.BlockSpec((tm, D_MODEL), lambda i, f: (i, 0)),
            pl.BlockSpec((1, D_MODEL), lambda i, f: (0, 0)),
            pl.BlockSpec((None, D_MODEL, tf), lambda i, f: (layer, 0, f)),
            pl.BlockSpec((None, D_MODEL, tf), lambda i, f: (layer, 0, f)),
            pl.BlockSpec((None, tf, D_MODEL), lambda i, f: (layer, f, 0)),
        ],
        out_specs=pl.BlockSpec((tm, D_MODEL), lambda i, f: (i, 0)),
        out_shape=jax.ShapeDtypeStruct(x2d.shape, F32),
        scratch_shapes=[pltpu.VMEM((tm, D_MODEL), BF16)],
        compiler_params=pltpu.CompilerParams(
            dimension_semantics=("arbitrary", "arbitrary"), vmem_limit_bytes=VMEM_LIMIT_BYTES),
        name="dense_ffn",
    )(x2d, g, wg, wu, wd)


MOE_TT = 256
MOE_TM = 512
ROW_ALIGN = 16
MOE_WIN = MOE_TT + ROW_ALIGN
DISPATCH_FAST_WIN = 144
DISPATCH_FAST_FILL = DISPATCH_FAST_WIN - ROW_ALIGN
COMBINE_FAST_WIN = 144
REGION_SLACK = MOE_WIN
ROUTE_ROWS = 8
ROUTER_ROWS = 1024


def _moe_ffn_kernel(te_ref, tb_ref, nu_ref, x_ref, wg_ref, wu_ref, wd_ref, ys_hbm, o_ref, acc_ref):
    del te_ref, tb_ref, ys_hbm
    f = pl.program_id(1)
    used = pl.program_id(0) < nu_ref[0]

    @pl.when(jnp.logical_and(used, f == 0))
    def _():
        acc_ref[...] = jnp.zeros(acc_ref.shape, F32)

    @pl.when(used)
    def _():
        _swiglu_accumulate(x_ref[...], wg_ref, wu_ref, wd_ref, acc_ref)

    @pl.when(jnp.logical_and(used, f == pl.num_programs(1) - 1))
    def _():
        o_ref[...] = acc_ref[...].astype(BF16)


def _moe_ffn(xs, tile_expert, tile_block, n_used, layer, wg, wu, wd, ys):
    p = xs.shape[0]
    tm = MOE_TM
    tf = D_FF // 2
    nf = D_FF // tf
    max_tiles = tile_expert.shape[0]

    def f_eff(i, f, nu):
        return jnp.where(i < nu[0], f, nf - 1)

    grid_spec = pltpu.PrefetchScalarGridSpec(
        num_scalar_prefetch=3,
        grid=(max_tiles, nf),
        in_specs=[
            pl.BlockSpec((tm, D_MODEL), lambda i, f, te, tb, nu: (tb[i], 0)),
            pl.BlockSpec((None, None, D_MODEL, tf), lambda i, f, te, tb, nu: (layer, te[i], 0, f_eff(i, f, nu))),
            pl.BlockSpec((None, None, D_MODEL, tf), lambda i, f, te, tb, nu: (layer, te[i], 0, f_eff(i, f, nu))),
            pl.BlockSpec((None, None, tf, D_MODEL), lambda i, f, te, tb, nu: (layer, te[i], f_eff(i, f, nu), 0)),
            pl.BlockSpec(memory_space=pl.ANY),
        ],
        out_specs=pl.BlockSpec((tm, D_MODEL), lambda i, f, te, tb, nu: (tb[i], 0)),
        scratch_shapes=[pltpu.VMEM((tm, D_MODEL), F32)],
    )
    return pl.pallas_call(
        _moe_ffn_kernel,
        grid_spec=grid_spec,
        out_shape=jax.ShapeDtypeStruct((p, D_MODEL), BF16),
        input_output_aliases={7: 0},
        compiler_params=pltpu.CompilerParams(
            dimension_semantics=("arbitrary", "arbitrary"), vmem_limit_bytes=VMEM_LIMIT_BYTES),
        name="moe_ffn",
    )(tile_expert, tile_block, n_used, xs, wg, wu, wd, ys)


def _router_kernel(x_ref, g_ref, wh_ref, wl_ref, b_ref, ltri_ref, hn_ref, rc_ref, rt_ref, cnt_ref):
    x = x_ref[...]
    h = x * _rms_scale(x, D_MODEL) * g_ref[...]
    hh = h.astype(BF16)
    hn_ref[...] = hh
    hl = (h - hh.astype(F32)).astype(BF16)
    wh = wh_ref[...]
    logits = (jnp.dot(hh, wh, preferred_element_type=F32)
              + jnp.dot(hl, wh, preferred_element_type=F32)
              + jnp.dot(hh, wl_ref[...], preferred_element_type=F32)) + b_ref[...]
    lane = lax.broadcasted_iota(jnp.int32, (MOE_TT, ROUTER_PAD), 1)
    for blk in range(x.shape[0] // MOE_TT):
        rows = slice(blk * MOE_TT, (blk + 1) * MOE_TT)
        lg = logits[rows]
        v1 = jnp.max(lg, axis=-1, keepdims=True)
        i1 = jnp.min(jnp.where(lg == v1, lane, ROUTER_PAD), axis=-1, keepdims=True)
        rest = jnp.where(lane == i1, -jnp.inf, lg)
        v2 = jnp.max(rest, axis=-1, keepdims=True)
        i2 = jnp.min(jnp.where(rest == v2, lane, ROUTER_PAD), axis=-1, keepdims=True)
        e2 = jnp.exp(v2 - v1)
        den = 1.0 + e2
        chosen = jnp.logical_or(lane == i1, lane == i2)
        cum = jnp.dot(ltri_ref[...], jnp.where(chosen, 1.0, 0.0).astype(BF16), preferred_element_type=F32)
        r1 = jnp.sum(jnp.where(lane == i1, cum, 0.0), axis=-1, keepdims=True) - 1.0
        r2 = jnp.sum(jnp.where(lane == i2, cum, 0.0), axis=-1, keepdims=True) - 1.0
        rc = jnp.zeros(lg.shape, F32)
        for k, col in enumerate((i1.astype(F32), i2.astype(F32), r1, r2, 1.0 / den, e2 / den)):
            rc = jnp.where(lane == k, col, rc)
        rc_ref[rows, :] = rc
        rt_ref[:, rows] = rc.T[:ROUTE_ROWS, :]
        cnt_ref[blk] = cum[MOE_TT - 1:MOE_TT, :].astype(jnp.int32)


def _router(x2d, g, w_hi, w_lo, b_pad, ltri):
    n = x2d.shape[0]
    tr = min(n, ROUTER_ROWS)
    tiles = tr // MOE_TT
    return pl.pallas_call(
        _router_kernel,
        grid=(n // tr,),
        in_specs=[
            pl.BlockSpec((tr, D_MODEL), lambda i: (i, 0)),
            pl.BlockSpec((1, D_MODEL), lambda i: (0, 0)),
            pl.BlockSpec((D_MODEL, ROUTER_PAD), lambda i: (0, 0)),
            pl.BlockSpec((D_MODEL, ROUTER_PAD), lambda i: (0, 0)),
            pl.BlockSpec((1, ROUTER_PAD), lambda i: (0, 0)),
            pl.BlockSpec((MOE_TT, MOE_TT), lambda i: (0, 0)),
        ],
        out_specs=[
            pl.BlockSpec((tr, D_MODEL), lambda i: (i, 0)),
            pl.BlockSpec((tr, ROUTER_PAD), lambda i: (i, 0)),
            pl.BlockSpec((ROUTE_ROWS, tr), lambda i: (0, i)),
            pl.BlockSpec((tiles, 1, ROUTER_PAD), lambda i: (i, 0, 0)),
        ],
        out_shape=[jax.ShapeDtypeStruct((n, D_MODEL), BF16),
                   jax.ShapeDtypeStruct((n, ROUTER_PAD), F32),
                   jax.ShapeDtypeStruct((ROUTE_ROWS, n), F32),
                   jax.ShapeDtypeStruct((n // MOE_TT, 1, ROUTER_PAD), jnp.int32)],
        compiler_params=pltpu.CompilerParams(
            dimension_semantics=("arbitrary",), vmem_limit_bytes=VMEM_LIMIT_BYTES),
        name="router",
    )(x2d, g, w_hi, w_lo, b_pad, ltri)


def _routing_tables(cnt, n):
    i32 = jnp.int32
    nt = n // MOE_TT
    cnt = cnt.reshape(nt, ROUTER_PAD)[:, :N_EXPERTS]
    total = jnp.sum(cnt, axis=0)
    region = ((total + REGION_SLACK + MOE_TM - 1) // MOE_TM) * MOE_TM
    start = jnp.cumsum(region) - region
    first = start[None, :] + jnp.cumsum(cnt, axis=0) - cnt
    a = jnp.concatenate([first, (start + total)[None, :]], axis=0).reshape(-1).astype(i32)
    ntile = (total + MOE_TM - 1) // MOE_TM
    tend = jnp.cumsum(ntile)
    n_used = tend[-1]
    max_tiles = 2 * n // MOE_TM + N_EXPERTS
    tile = jnp.minimum(jnp.arange(max_tiles, dtype=i32), n_used - 1)
    tile_expert = jnp.minimum(jnp.sum((tend[None, :] <= tile[:, None]).astype(i32), axis=1), N_EXPERTS - 1)
    tile_block = start[tile_expert] // MOE_TM + tile - (tend - ntile)[tile_expert]
    fill = jnp.max(jnp.bitwise_and(first, ROW_ALIGN - 1) + cnt, axis=1)
    fast_dispatch = jnp.logical_and(fill <= DISPATCH_FAST_FILL, jnp.arange(nt) < nt - 1).astype(i32)
    fast_combine = (fill <= COMBINE_FAST_WIN).astype(i32)
    return (a, tile_expert.astype(i32), tile_block.astype(i32), n_used.reshape(1).astype(i32),
            fast_dispatch, fast_combine)


def _sorted_rows(n):
    return -(-(2 * n + N_EXPERTS * (REGION_SLACK + MOE_TM)) // MOE_TM) * MOE_TM


def _window(a_ref, t, e):
    a = a_ref[t * N_EXPERTS + e]
    off = jnp.bitwise_and(a, ROW_ALIGN - 1)
    return pl.multiple_of(a - off, ROW_ALIGN), off


def _expert_slot(e, e1, e2, r1, r2, off):
    d = jnp.where(e1 == e, r1, jnp.where(e2 == e, r2, -1.0))
    return jnp.where(d >= 0.0, d + off.astype(F32), -1.0).astype(jnp.int32)


def _dispatch_kernel(a_ref, fast_ref, hn_ref, rt_ref, zeros_hbm, xs_hbm, carry, stage, sem):
    del zeros_hbm
    t = pl.program_id(0)
    nt = pl.num_programs(0)
    slot = t % 2

    def window_copy(sl, e, base, win):
        return pltpu.make_async_copy(stage.at[sl, e, 0:win], xs_hbm.at[pl.ds(base, win)], sem)

    @pl.when(t == 0)
    def _():
        carry[...] = jnp.zeros(carry.shape, F32)

    def stage_and_send(win, prev_win_is_fast):
        hn = hn_ref[...]
        rt = rt_ref[...]
        e1, e2, r1, r2 = rt[0:1], rt[1:2], rt[2:3], rt[3:4]
        row = lax.broadcasted_iota(jnp.int32, (win, MOE_TT), 0)
        head_row = lax.broadcasted_iota(jnp.int32, (ROW_ALIGN, 1), 0)
        bases = []
        for e in range(N_EXPERTS):
            base, off = _window(a_ref, t, e)
            bases.append(base)
            slot_of = _expert_slot(e, e1, e2, r1, r2, off)
            sel = jnp.where(row == slot_of, 1.0, 0.0).astype(BF16)
            rows = jnp.dot(sel, hn, preferred_element_type=F32)
            head = jnp.where(head_row < off, carry[e], rows[0:ROW_ALIGN])
            stage[slot, e, 0:ROW_ALIGN, :] = head.astype(BF16)
            stage[slot, e, ROW_ALIGN:win, :] = rows[ROW_ALIGN:].astype(BF16)
            filled = off + a_ref[(t + 1) * N_EXPERTS + e] - a_ref[t * N_EXPERTS + e]
            last_group = pl.multiple_of(lax.shift_right_logical(filled, 4) * ROW_ALIGN, ROW_ALIGN)
            carry[e] = stage[slot, e, pl.ds(last_group, ROW_ALIGN), :].astype(F32)

        for was_fast, prev_win in ((1, DISPATCH_FAST_WIN), (0, MOE_WIN)):
            @pl.when(jnp.logical_and(t > 0, prev_win_is_fast == was_fast))
            def _():
                for e in range(N_EXPERTS):
                    window_copy(1 - slot, e, 0, prev_win).wait()

        for e in range(N_EXPERTS):
            window_copy(slot, e, bases[e], win).start()

    is_fast = fast_ref[t]
    prev_fast = fast_ref[jnp.maximum(t - 1, 0)]

    @pl.when(is_fast == 1)
    def _():
        stage_and_send(DISPATCH_FAST_WIN, prev_fast)

    @pl.when(is_fast == 0)
    def _():
        stage_and_send(MOE_WIN, prev_fast)

    @pl.when(t == nt - 1)
    def _():
        for e in range(N_EXPERTS):
            window_copy(slot, e, 0, MOE_WIN).wait()


def _dispatch(hn, route_t, a, fast, zeros):
    n = hn.shape[0]
    tt = MOE_TT
    grid_spec = pltpu.PrefetchScalarGridSpec(
        num_scalar_prefetch=2,
        grid=(n // tt,),
        in_specs=[
            pl.BlockSpec((tt, D_MODEL), lambda t, a, fs: (t, 0)),
            pl.BlockSpec((ROUTE_ROWS, tt), lambda t, a, fs: (0, t)),
            pl.BlockSpec(memory_space=pl.ANY),
        ],
        out_specs=pl.BlockSpec(memory_space=pl.ANY),
        scratch_shapes=[
            pltpu.VMEM((N_EXPERTS, ROW_ALIGN, D_MODEL), F32),
            pltpu.VMEM((2, N_EXPERTS, MOE_WIN, D_MODEL), BF16),
            pltpu.SemaphoreType.DMA(()),
        ],
    )
    return pl.pallas_call(
        _dispatch_kernel,
        grid_spec=grid_spec,
        out_shape=jax.ShapeDtypeStruct(zeros.shape, BF16),
        input_output_aliases={4: 0},
        compiler_params=pltpu.CompilerParams(
            dimension_semantics=("arbitrary",), vmem_limit_bytes=VMEM_LIMIT_BYTES),
        name="moe_dispatch",
    )(a, fast, hn, route_t, zeros)


def _combine_kernel(a_ref, fast_ref, x_ref, rc_ref, ys_hbm, o_ref, ybuf, sem):
    t = pl.program_id(0)
    nt = pl.num_programs(0)
    slot = t % 2
    window_sizes = ((1, COMBINE_FAST_WIN), (0, MOE_WIN))

    def window_copy(tile, sl, e, win):
        base, _ = _window(a_ref, tile, e)
        return pltpu.make_async_copy(ys_hbm.at[pl.ds(base, win)], ybuf.at[sl, e, 0:win], sem.at[sl])

    def start_windows(tile, sl):
        for flag, win in window_sizes:
            @pl.when(fast_ref[tile] == flag)
            def _():
                for e in range(N_EXPERTS):
                    window_copy(tile, sl, e, win).start()

    @pl.when(t == 0)
    def _():
        start_windows(0, 0)

    @pl.when(t + 1 < nt)
    def _():
        start_windows(t + 1, 1 - slot)

    rc = rc_ref[...]
    e1, e2, r1, r2, g1, g2 = (rc[:, k:k + 1] for k in range(6))

    def gather_add(win):
        for e in range(N_EXPERTS):
            window_copy(t, slot, e, win).wait()
        lane = lax.broadcasted_iota(jnp.int32, (MOE_TT, win), 1)
        acc = jnp.zeros((MOE_TT, D_MODEL), F32)
        for e in range(N_EXPERTS):
            _, off = _window(a_ref, t, e)
            slot_of = _expert_slot(e, e1, e2, r1, r2, off)
            sel = jnp.where(lane == slot_of, 1.0, 0.0).astype(BF16)
            gate = jnp.where(e1 == e, g1, jnp.where(e2 == e, g2, 0.0))
            acc = acc + gate * jnp.dot(sel, ybuf[slot, e, 0:win, :], preferred_element_type=F32)
        o_ref[...] = x_ref[...] + acc

    for flag, win in window_sizes:
        @pl.when(fast_ref[t] == flag)
        def _():
            gather_add(win)


def _combine(x2d, route_c, ys, a, fast):
    n = x2d.shape[0]
    tt = MOE_TT
    grid_spec = pltpu.PrefetchScalarGridSpec(
        num_scalar_prefetch=2,
        grid=(n // tt,),
        in_specs=[
            pl.BlockSpec((tt, D_MODEL), lambda t, a, fs: (t, 0)),
            pl.BlockSpec((tt, ROUTER_PAD), lambda t, a, fs: (t, 0)),
            pl.BlockSpec(memory_space=pl.ANY),
        ],
        out_specs=pl.BlockSpec((tt, D_MODEL), lambda t, a, fs: (t, 0)),
        scratch_shapes=[
            pltpu.VMEM((2, N_EXPERTS, MOE_WIN, D_MODEL), BF16),
            pltpu.SemaphoreType.DMA((2,)),
        ],
    )
    return pl.pallas_call(
        _combine_kernel,
        grid_spec=grid_spec,
        out_shape=jax.ShapeDtypeStruct((n, D_MODEL), F32),
        compiler_params=pltpu.CompilerParams(
            dimension_semantics=("arbitrary",), vmem_limit_bytes=VMEM_LIMIT_BYTES),
        name="moe_combine",
    )(a, fast, x2d, route_c, ys)


def _block_diag_ones(width):
    r = jnp.arange(width) // HEAD_DIM
    return (r[:, None] == r[None, :]).astype(BF16)


def _swa_bias():
    qi = jnp.arange(PAIR)[:, None]
    kj = jnp.arange(WIN)[None, :]
    dist = jnp.abs(qi + HALO - kj).astype(F32)
    kc = kj // CHUNK
    qc = qi // CHUNK
    visible = jnp.logical_and(kc >= qc, kc <= qc + 2)
    slopes = jnp.asarray([2.0 ** (-8.0 * (i + 1) / SWA_HEADS) for i in range(SWA_HEADS)], F32)
    bias = -slopes[:, None, None] * dist[None]
    return jnp.where(visible[None], bias, -jnp.inf)


def kernel(x, mem, g_mix, w_in, g_q_swa, g_k_swa, sinks, conv_w, g_mem, w_mem_kv, g_q_mem, g_k_mem,
           g_out_swa, g_out_conv, g_out_mem, w_out, g_ffn, w_gate_dense, w_up_dense, w_down_dense,
           w_router, b_router, w_gate_moe, w_up_moe, w_down_moe):
    batch, seq, _ = x.shape
    depth = g_mix.shape[0]
    n = batch * seq
    scale = HEAD_DIM ** -0.5

    bd_qk = _block_diag_ones(QK_WIDTH)
    bd_mem = _block_diag_ones(MEM_WIDTH)
    bias = _swa_bias()
    gqk = jnp.concatenate([jnp.tile(g_q_swa * scale, (1, SWA_HEADS)),
                           jnp.tile(g_k_swa, (1, SWA_KV_HEADS))], axis=1)[:, None, :]
    gqm = jnp.tile(g_q_mem * scale, (1, MEM_HEADS))[:, None, :]
    gkm = jnp.tile(g_k_mem, (1, MEM_HEADS))[:, None, :]
    gout = jnp.concatenate([g_out_swa, g_out_conv, g_out_mem], axis=1)[:, None, :]

    w_in_b = w_in.astype(BF16)
    w_out_b = w_out.astype(BF16)
    mem_kv_all = _mem_kv(mem.reshape(-1, D_MODEL), g_mem[:, None, :], w_mem_kv.astype(BF16), gkm, bd_mem)

    wr = jnp.pad(w_router, ((0, 0), (0, 0), (0, ROUTER_PAD - N_EXPERTS)))
    wr_hi = wr.astype(BF16)
    wr_lo = (wr - wr_hi.astype(F32)).astype(BF16)
    br = jnp.pad(b_router, ((0, 0), (0, ROUTER_PAD - N_EXPERTS)), constant_values=-jnp.inf)[:, None, :]
    ltri = (jnp.arange(MOE_TT)[:, None] >= jnp.arange(MOE_TT)[None, :]).astype(BF16)

    wg_dense, wu_dense, wd_dense = (w.astype(BF16) for w in (w_gate_dense, w_up_dense, w_down_dense))
    wg_moe, wu_moe, wd_moe = (w.astype(BF16) for w in (w_gate_moe, w_up_moe, w_down_moe))

    xs = x.reshape(n, D_MODEL)
    for l in range(depth):
        xs = _mixer(xs, batch, seq, sinks[l], g_mix[l][None], w_in_b[l], gqk[l], gqm[l], bd_qk, bd_mem,
                    bias, conv_w[l], [t[l] for t in mem_kv_all], gout[l], w_out_b[l])
        i = l // 2
        gf = g_ffn[l][None]
        if l % 2 == 0:
            xs = _dense_ffn(xs, gf, i, wg_dense, wu_dense, wd_dense)
        else:
            hn, route_c, route_t, cnt = _router(xs, gf, wr_hi[i], wr_lo[i], br[i], ltri)
            a, tile_expert, tile_block, n_used, fast_d, fast_c = _routing_tables(cnt, n)
            sorted_shape = (_sorted_rows(n), D_MODEL)
            x_sorted = _dispatch(hn, route_t, a, fast_d, jnp.zeros(sorted_shape, BF16))
            y_sorted = _moe_ffn(x_sorted, tile_expert, tile_block, n_used, i, wg_moe, wu_moe, wd_moe,
                                jnp.zeros(sorted_shape, BF16))
            xs = _combine(xs, route_c, y_sorted, a, fast_c)
    return xs.reshape(batch, seq, D_MODEL)
```

```python
import functools

import jax
import jax.numpy as jnp
from jax import lax
from jax.experimental import pallas as pl
from jax.experimental.pallas import tpu as pltpu

F32 = jnp.float32
BF16 = jnp.bfloat16

D_MODEL = 1024
CHUNK = 64
HEAD_DIM = 64
SWA_HEADS = 8
SWA_KV_HEADS = 2
SWA_GROUP = SWA_HEADS // SWA_KV_HEADS
CONV_WIDTH = 256
CONV_K = 3
MEM_HEADS = 4
SWA_WIDTH = SWA_HEADS * HEAD_DIM
KV_WIDTH = SWA_KV_HEADS * HEAD_DIM
MEM_WIDTH = MEM_HEADS * HEAD_DIM
QK_WIDTH = SWA_WIDTH + KV_WIDTH
IN_WIDTH = SWA_WIDTH + 2 * KV_WIDTH + 3 * CONV_WIDTH + MEM_WIDTH
D_FF = 3584
N_EXPERTS = 8
EPS = 1e-6

PAIR = 2 * CHUNK
WIN = 4 * CHUNK
HALO = WIN - PAIR
CONV_HALO = 8
LANES = 128
ROUTER_PAD = LANES

VMEM_LIMIT_BYTES = 56 * 1024 * 1024

NT_DIMS = (((1,), (1,)), ((), ()))


def _rms_scale(x, width):
    return lax.rsqrt(jnp.sum(x * x, axis=-1, keepdims=True) * (1.0 / width) + EPS)


def _head_rms(t, bd_ref):
    ss = jnp.dot((t * t).astype(BF16), bd_ref[...], preferred_element_type=F32)
    return lax.rsqrt(ss * (1.0 / HEAD_DIM) + EPS)


def _low_half(shape):
    return lax.broadcasted_iota(jnp.int32, shape, len(shape) - 1) % LANES < HEAD_DIM


def _softmax_rows(s, sink=None):
    m = jnp.max(s, axis=-1, keepdims=True)
    if sink is not None:
        m = jnp.maximum(m, sink)
    p = jnp.exp(s - m)
    den = jnp.sum(p, axis=-1, keepdims=True)
    if sink is not None:
        den = den + jnp.exp(sink - m)
    return (p * (1.0 / den)).astype(BF16)


def _memkv_kernel(mem_ref, g_ref, w_ref, gk_ref, bd_ref, mka_ref, mkb_ref, mva_ref, mvb_ref):
    m = mem_ref[...]
    hm = (m * _rms_scale(m, D_MODEL) * g_ref[...]).astype(BF16)
    kv = jnp.dot(hm, w_ref[...], preferred_element_type=F32)
    k = kv[:, :MEM_WIDTH]
    k = k * _head_rms(k, bd_ref) * gk_ref[...]
    v = kv[:, MEM_WIDTH:]
    low = _low_half(k.shape)
    mka_ref[...] = jnp.where(low, k, 0.0).astype(BF16)
    mkb_ref[...] = jnp.where(low, 0.0, k).astype(BF16)
    mva_ref[...] = jnp.where(low, v, 0.0).astype(BF16)
    mvb_ref[...] = jnp.where(low, 0.0, v).astype(BF16)


def _mem_kv(mem2d, g_mem, w_mem_kv, gk_mem, bd_mem):
    depth = g_mem.shape[0]
    rows = mem2d.shape[0]
    tr = min(rows, 512)
    out = jax.ShapeDtypeStruct((depth, rows, MEM_WIDTH), BF16)
    out_spec = pl.BlockSpec((None, tr, MEM_WIDTH), lambda l, i: (l, i, 0))
    return pl.pallas_call(
        _memkv_kernel,
        grid=(depth, rows // tr),
        in_specs=[
            pl.BlockSpec((tr, D_MODEL), lambda l, i: (i, 0)),
            pl.BlockSpec((None, 1, D_MODEL), lambda l, i: (l, 0, 0)),
            pl.BlockSpec((None, D_MODEL, 2 * MEM_WIDTH), lambda l, i: (l, 0, 0)),
            pl.BlockSpec((None, 1, MEM_WIDTH), lambda l, i: (l, 0, 0)),
            pl.BlockSpec((MEM_WIDTH, MEM_WIDTH), lambda l, i: (0, 0)),
        ],
        out_specs=[out_spec] * 4,
        out_shape=[out] * 4,
        compiler_params=pltpu.CompilerParams(
            dimension_semantics=("arbitrary", "arbitrary"), vmem_limit_bytes=VMEM_LIMIT_BYTES),
        name="mem_kv",
    )(mem2d, g_mem, w_mem_kv, gk_mem, bd_mem)


MEM_ROWS = 256


def _mixer_kernel(sinks_ref, x_ref, gmix_ref, win_ref, gqk_ref, gqm_ref, bdqk_ref, bdm_ref, bias_ref,
                  convw_ref, mka_ref, mkb_ref, mva_ref, mvb_ref, gout_ref, wout_ref, o_ref,
                  kbuf, vbuf, zbuf, yswa, ymem, *, ts):
    s_idx = pl.program_id(1)
    x = x_ref[...]
    h = (x * _rms_scale(x, D_MODEL) * gmix_ref[...]).astype(BF16)
    proj = jnp.dot(h, win_ref[...], preferred_element_type=F32)

    @pl.when(s_idx == 0)
    def _():
        kbuf[:, 0:HALO, :] = jnp.zeros((2 * SWA_KV_HEADS, HALO, LANES), BF16)
        vbuf[:, 0:HALO, :] = jnp.zeros((2 * SWA_KV_HEADS, HALO, LANES), BF16)
        zbuf[0:CONV_HALO, :] = jnp.zeros((CONV_HALO, CONV_WIDTH), F32)

    qk = proj[:, :QK_WIDTH]
    qk = qk * _head_rms(qk, bdqk_ref) * gqk_ref[...]
    q = qk[:, :SWA_WIDTH].astype(BF16)
    low = _low_half((ts, KV_WIDTH))
    for buf, t in ((kbuf, qk[:, SWA_WIDTH:]), (vbuf, proj[:, QK_WIDTH:QK_WIDTH + KV_WIDTH])):
        swapped = pltpu.roll(t, HEAD_DIM, 1)
        buf[0, HALO:HALO + ts, :] = jnp.where(low, t, 0.0).astype(BF16)
        buf[1, HALO:HALO + ts, :] = jnp.where(low, 0.0, swapped).astype(BF16)
        buf[2, HALO:HALO + ts, :] = jnp.where(low, swapped, 0.0).astype(BF16)
        buf[3, HALO:HALO + ts, :] = jnp.where(low, 0.0, t).astype(BF16)

    key_lane = lax.broadcasted_iota(jnp.int32, (1, WIN), 1)
    first_keys_valid = jnp.logical_or(key_lane >= HALO, s_idx > 0)

    for j in range(ts // PAIR):
        r0 = j * PAIR
        for kh in range(SWA_KV_HEADS):
            c0 = kh * SWA_GROUP * HEAD_DIM
            qg = jnp.concatenate([q[r0:r0 + PAIR, c0:c0 + LANES],
                                  q[r0:r0 + PAIR, c0 + LANES:c0 + 2 * LANES]], axis=0)
            probs = []
            for half in range(2):
                kk = kbuf[2 * kh + half, r0:r0 + WIN, :]
                s_all = lax.dot_general(qg, kk, NT_DIMS, preferred_element_type=F32)
                per_pair = []
                for pr in range(2):
                    hd = kh * SWA_GROUP + 2 * pr + half
                    sg = s_all[pr * PAIR:(pr + 1) * PAIR] + bias_ref[hd]
                    if j == 0:
                        sg = jnp.where(first_keys_valid, sg, -jnp.inf)
                    per_pair.append(_softmax_rows(sg, sinks_ref[hd]))
                probs.append(jnp.concatenate(per_pair, axis=0))
            o = (jnp.dot(probs[0], vbuf[2 * kh, r0:r0 + WIN, :], preferred_element_type=F32)
                 + jnp.dot(probs[1], vbuf[2 * kh + 1, r0:r0 + WIN, :], preferred_element_type=F32))
            yswa[r0:r0 + PAIR, c0:c0 + LANES] = o[0:PAIR]
            yswa[r0:r0 + PAIR, c0 + LANES:c0 + 2 * LANES] = o[PAIR:]

    kbuf[:, 0:HALO, :] = kbuf[:, ts:ts + HALO, :]
    vbuf[:, 0:HALO, :] = vbuf[:, ts:ts + HALO, :]

    c0 = QK_WIDTH + KV_WIDTH
    gate_b = proj[:, c0:c0 + CONV_WIDTH]
    z = proj[:, c0 + CONV_WIDTH:c0 + 2 * CONV_WIDTH] * proj[:, c0 + 2 * CONV_WIDTH:c0 + 3 * CONV_WIDTH]
    zbuf[CONV_HALO:CONV_HALO + ts, :] = z
    z1 = zbuf[CONV_HALO - 1:CONV_HALO - 1 + ts, :]
    z2 = zbuf[CONV_HALO - 2:CONV_HALO - 2 + ts, :]
    cw = convw_ref[...]
    y_conv = gate_b * (cw[0:1] * z2 + cw[1:2] * z1 + cw[2:3] * z)
    zbuf[0:CONV_HALO, :] = zbuf[ts:ts + CONV_HALO, :]

    qm = proj[:, IN_WIDTH - MEM_WIDTH:]
    qm = (qm * _head_rms(qm, bdm_ref) * gqm_ref[...]).astype(BF16)
    mem_rows = min(ts, MEM_ROWS)
    for pr in range(MEM_HEADS // 2):
        cols = slice(pr * LANES, (pr + 1) * LANES)
        for rb in range(ts // mem_rows):
            rows = slice(rb * mem_rows, (rb + 1) * mem_rows)
            qp = qm[rows, cols]
            pa = _softmax_rows(lax.dot_general(qp, mka_ref[:, cols], NT_DIMS, preferred_element_type=F32))
            pb = _softmax_rows(lax.dot_general(qp, mkb_ref[:, cols], NT_DIMS, preferred_element_type=F32))
            ymem[rows, cols] = (jnp.dot(pa, mva_ref[:, cols], preferred_element_type=F32)
                                + jnp.dot(pb, mvb_ref[:, cols], preferred_element_type=F32))

    gout = gout_ref[...]
    ys = yswa[...]
    ym = ymem[...]
    a = (ys * _rms_scale(ys, SWA_WIDTH) * gout[:, :SWA_WIDTH]).astype(BF16)
    b = (y_conv * _rms_scale(y_conv, CONV_WIDTH) * gout[:, SWA_WIDTH:SWA_WIDTH + CONV_WIDTH]).astype(BF16)
    c = (ym * _rms_scale(ym, MEM_WIDTH) * gout[:, SWA_WIDTH + CONV_WIDTH:]).astype(BF16)
    out = x + jnp.dot(a, wout_ref[0:SWA_WIDTH, :], preferred_element_type=F32)
    out = out + jnp.dot(b, wout_ref[SWA_WIDTH:SWA_WIDTH + CONV_WIDTH, :], preferred_element_type=F32)
    out = out + jnp.dot(c, wout_ref[SWA_WIDTH + CONV_WIDTH:, :], preferred_element_type=F32)
    o_ref[...] = out


def _mixer(x2d, batch, seq, sinks, gmix, w_in, gqk, gqm, bd_qk, bd_mem, bias, conv_w, mem_kv, gout, w_out):
    ts = min(seq, 1024)
    nseq = seq // ts
    mem_len = mem_kv[0].shape[0] // batch
    const = lambda b, s, sk: (0, 0)
    mem_spec = pl.BlockSpec((mem_len, MEM_WIDTH), lambda b, s, sk: (b, 0))
    grid_spec = pltpu.PrefetchScalarGridSpec(
        num_scalar_prefetch=1,
        grid=(batch, nseq),
        in_specs=[
            pl.BlockSpec((ts, D_MODEL), lambda b, s, sk: (b * nseq + s, 0)),
            pl.BlockSpec((1, D_MODEL), const),
            pl.BlockSpec((D_MODEL, IN_WIDTH), const),
            pl.BlockSpec((1, QK_WIDTH), const),
            pl.BlockSpec((1, MEM_WIDTH), const),
            pl.BlockSpec((QK_WIDTH, QK_WIDTH), const),
            pl.BlockSpec((MEM_WIDTH, MEM_WIDTH), const),
            pl.BlockSpec((SWA_HEADS, PAIR, WIN), lambda b, s, sk: (0, 0, 0)),
            pl.BlockSpec((CONV_K, CONV_WIDTH), const),
            mem_spec, mem_spec, mem_spec, mem_spec,
            pl.BlockSpec((1, D_MODEL), const),
            pl.BlockSpec((D_MODEL, D_MODEL), const),
        ],
        out_specs=pl.BlockSpec((ts, D_MODEL), lambda b, s, sk: (b * nseq + s, 0)),
        scratch_shapes=[
            pltpu.VMEM((2 * SWA_KV_HEADS, ts + HALO, LANES), BF16),
            pltpu.VMEM((2 * SWA_KV_HEADS, ts + HALO, LANES), BF16),
            pltpu.VMEM((ts + CONV_HALO, CONV_WIDTH), F32),
            pltpu.VMEM((ts, SWA_WIDTH), F32),
            pltpu.VMEM((ts, MEM_WIDTH), F32),
        ],
    )
    return pl.pallas_call(
        functools.partial(_mixer_kernel, ts=ts),
        grid_spec=grid_spec,
        out_shape=jax.ShapeDtypeStruct(x2d.shape, F32),
        compiler_params=pltpu.CompilerParams(
            dimension_semantics=("arbitrary", "arbitrary"), vmem_limit_bytes=VMEM_LIMIT_BYTES),
        name="token_mixer",
    )(sinks, x2d, gmix, w_in, gqk, gqm, bd_qk, bd_mem, bias, conv_w, *mem_kv, gout, w_out)


FF_CHUNK = 256


def _swiglu_accumulate(h, wg_ref, wu_ref, wd_ref, acc_ref):
    tf = wg_ref.shape[-1]
    for c in range(tf // FF_CHUNK):
        sl = slice(c * FF_CHUNK, (c + 1) * FF_CHUNK)
        gate = jnp.dot(h, wg_ref[:, sl], preferred_element_type=F32)
        up = jnp.dot(h, wu_ref[:, sl], preferred_element_type=F32)
        act = (gate * jax.nn.sigmoid(gate) * up).astype(BF16)
        acc_ref[...] += jnp.dot(act, wd_ref[sl, :], preferred_element_type=F32)


def _dense_ffn_kernel(x_ref, g_ref, wg_ref, wu_ref, wd_ref, o_ref, h_ref):
    @pl.when(pl.program_id(1) == 0)
    def _():
        x = x_ref[...]
        h_ref[...] = (x * _rms_scale(x, D_MODEL) * g_ref[...]).astype(BF16)
        o_ref[...] = x

    _swiglu_accumulate(h_ref[...], wg_ref, wu_ref, wd_ref, o_ref)


def _dense_ffn(x2d, g, layer, wg, wu, wd):
    n = x2d.shape[0]
    tm = min(n, 1024)
    tf = D_FF // 2
    return pl.pallas_call(
        _dense_ffn_kernel,
        grid=(n // tm, D_FF // tf),
        in_specs=[
            pl.BlockSpec((tm, D_MODEL), lambda i, f: (i, 0)),
            pl.BlockSpec((1, D_MODEL), lambda i, f: (0, 0)),
            pl.BlockSpec((None, D_MODEL, tf), lambda i, f: (layer, 0, f)),
            pl.BlockSpec((None, D_MODEL, tf), lambda i, f: (layer, 0, f)),
            pl.BlockSpec((None, tf, D_MODEL), lambda i, f: (layer, f, 0)),
        ],
        out_specs=pl.BlockSpec((tm, D_MODEL), lambda i, f: (i, 0)),
        out_shape=jax.ShapeDtypeStruct(x2d.shape, F32),
        scratch_shapes=[pltpu.VMEM((tm, D_MODEL), BF16)],
        compiler_params=pltpu.CompilerParams(
            dimension_semantics=("arbitrary", "arbitrary"), vmem_limit_bytes=VMEM_LIMIT_BYTES),
        name="dense_ffn",
    )(x2d, g, wg, wu, wd)


MOE_TT = 256
MOE_TM = 512
ROW_ALIGN = 16
MOE_WIN = MOE_TT + ROW_ALIGN
DISPATCH_FAST_WIN = 144
DISPATCH_FAST_FILL = DISPATCH_FAST_WIN - ROW_ALIGN
COMBINE_FAST_WIN = LANES
REGION_SLACK = MOE_WIN
ROUTE_ROWS = 8
ROUTER_ROWS = 1024


def _moe_ffn_kernel(te_ref, tb_ref, nu_ref, x_ref, wg_ref, wu_ref, wd_ref, ys_hbm, o_ref, acc_ref):
    del te_ref, tb_ref, ys_hbm
    f = pl.program_id(1)
    used = pl.program_id(0) < nu_ref[0]

    @pl.when(jnp.logical_and(used, f == 0))
    def _():
        acc_ref[...] = jnp.zeros(acc_ref.shape, F32)

    @pl.when(used)
    def _():
        _swiglu_accumulate(x_ref[...], wg_ref, wu_ref, wd_ref, acc_ref)

    @pl.when(jnp.logical_and(used, f == pl.num_programs(1) - 1))
    def _():
        o_ref[...] = acc_ref[...].astype(BF16)


def _moe_ffn(xs, tile_expert, tile_block, n_used, layer, wg, wu, wd, ys):
    p = xs.shape[0]
    tm = MOE_TM
    tf = D_FF // 2
    nf = D_FF // tf
    max_tiles = tile_expert.shape[0]

    def f_eff(i, f, nu):
        return jnp.where(i < nu[0], f, nf - 1)

    grid_spec = pltpu.PrefetchScalarGridSpec(
        num_scalar_prefetch=3,
        grid=(max_tiles, nf),
        in_specs=[
            pl.BlockSpec((tm, D_MODEL), lambda i, f, te, tb, nu: (tb[i], 0)),
            pl.BlockSpec((None, None, D_MODEL, tf), lambda i, f, te, tb, nu: (layer, te[i], 0, f_eff(i, f, nu))),
            pl.BlockSpec((None, None, D_MODEL, tf), lambda i, f, te, tb, nu: (layer, te[i], 0, f_eff(i, f, nu))),
            pl.BlockSpec((None, None, tf, D_MODEL), lambda i, f, te, tb, nu: (layer, te[i], f_eff(i, f, nu), 0)),
            pl.BlockSpec(memory_space=pl.ANY),
        ],
        out_specs=pl.BlockSpec((tm, D_MODEL), lambda i, f, te, tb, nu: (tb[i], 0)),
        scratch_shapes=[pltpu.VMEM((tm, D_MODEL), F32)],
    )
    return pl.pallas_call(
        _moe_ffn_kernel,
        grid_spec=grid_spec,
        out_shape=jax.ShapeDtypeStruct((p, D_MODEL), BF16),
        input_output_aliases={7: 0},
        compiler_params=pltpu.CompilerParams(
            dimension_semantics=("arbitrary", "arbitrary"), vmem_limit_bytes=VMEM_LIMIT_BYTES),
        name="moe_ffn",
    )(tile_expert, tile_block, n_used, xs, wg, wu, wd, ys)


def _router_kernel(x_ref, g_ref, wh_ref, wl_ref, b_ref, ltri_ref, hn_ref, rc_ref, rt_ref, cnt_ref, xs0_ref, ys0_ref):
    xs0_ref[...] = jnp.zeros(xs0_ref.shape, BF16)
    ys0_ref[...] = jnp.zeros(ys0_ref.shape, BF16)
    x = x_ref[...]
    h = x * _rms_scale(x, D_MODEL) * g_ref[...]
    hh = h.astype(BF16)
    hn_ref[...] = hh
    hl = (h - hh.astype(F32)).astype(BF16)
    wh = wh_ref[...]
    logits = (jnp.dot(hh, wh, preferred_element_type=F32)
              + jnp.dot(hl, wh, preferred_element_type=F32)
              + jnp.dot(hh, wl_ref[...], preferred_element_type=F32)) + b_ref[...]
    lane = lax.broadcasted_iota(jnp.int32, (MOE_TT, ROUTER_PAD), 1)
    for blk in range(x.shape[0] // MOE_TT):
        rows = slice(blk * MOE_TT, (blk + 1) * MOE_TT)
        lg = logits[rows]
        v1 = jnp.max(lg, axis=-1, keepdims=True)
        i1 = jnp.min(jnp.where(lg == v1, lane, ROUTER_PAD), axis=-1, keepdims=True)
        rest = jnp.where(lane == i1, -jnp.inf, lg)
        v2 = jnp.max(rest, axis=-1, keepdims=True)
        i2 = jnp.min(jnp.where(rest == v2, lane, ROUTER_PAD), axis=-1, keepdims=True)
        e2 = jnp.exp(v2 - v1)
        den = 1.0 + e2
        chosen = jnp.logical_or(lane == i1, lane == i2)
        cum = jnp.dot(ltri_ref[...], jnp.where(chosen, 1.0, 0.0).astype(BF16), preferred_element_type=F32)
        r1 = jnp.sum(jnp.where(lane == i1, cum, 0.0), axis=-1, keepdims=True) - 1.0
        r2 = jnp.sum(jnp.where(lane == i2, cum, 0.0), axis=-1, keepdims=True) - 1.0
        rc = jnp.zeros(lg.shape, F32)
        for k, col in enumerate((i1.astype(F32), i2.astype(F32), r1, r2, 1.0 / den, e2 / den)):
            rc = jnp.where(lane == k, col, rc)
        rc_ref[rows, :] = rc
        rt_ref[:, rows] = rc.T[:ROUTE_ROWS, :]
        cnt_ref[blk] = cum[MOE_TT - 1:MOE_TT, :].astype(jnp.int32)


def _router(x2d, g, w_hi, w_lo, b_pad, ltri):
    n = x2d.shape[0]
    tr = min(n, ROUTER_ROWS)
    tiles = tr // MOE_TT
    sorted_rows = _sorted_rows(n)
    zero_rows = sorted_rows // (n // tr)
    return pl.pallas_call(
        _router_kernel,
        grid=(n // tr,),
        in_specs=[
            pl.BlockSpec((tr, D_MODEL), lambda i: (i, 0)),
            pl.BlockSpec((1, D_MODEL), lambda i: (0, 0)),
            pl.BlockSpec((D_MODEL, ROUTER_PAD), lambda i: (0, 0)),
            pl.BlockSpec((D_MODEL, ROUTER_PAD), lambda i: (0, 0)),
            pl.BlockSpec((1, ROUTER_PAD), lambda i: (0, 0)),
            pl.BlockSpec((MOE_TT, MOE_TT), lambda i: (0, 0)),
        ],
        out_specs=[
            pl.BlockSpec((tr, D_MODEL), lambda i: (i, 0)),
            pl.BlockSpec((tr, ROUTER_PAD), lambda i: (i, 0)),
            pl.BlockSpec((ROUTE_ROWS, tr), lambda i: (0, i)),
            pl.BlockSpec((tiles, 1, ROUTER_PAD), lambda i: (i, 0, 0)),
            pl.BlockSpec((zero_rows, D_MODEL), lambda i: (i, 0)),
            pl.BlockSpec((zero_rows, D_MODEL), lambda i: (i, 0)),
        ],
        out_shape=[jax.ShapeDtypeStruct((n, D_MODEL), BF16),
                   jax.ShapeDtypeStruct((n, ROUTER_PAD), F32),
                   jax.ShapeDtypeStruct((ROUTE_ROWS, n), F32),
                   jax.ShapeDtypeStruct((n // MOE_TT, 1, ROUTER_PAD), jnp.int32),
                   jax.ShapeDtypeStruct((sorted_rows, D_MODEL), BF16),
                   jax.ShapeDtypeStruct((sorted_rows, D_MODEL), BF16)],
        compiler_params=pltpu.CompilerParams(
            dimension_semantics=("arbitrary",), vmem_limit_bytes=VMEM_LIMIT_BYTES),
        name="router",
    )(x2d, g, w_hi, w_lo, b_pad, ltri)


def _routing_tables(cnt, n):
    i32 = jnp.int32
    nt = n // MOE_TT
    cnt = cnt.reshape(nt, ROUTER_PAD)[:, :N_EXPERTS]
    total = jnp.sum(cnt, axis=0)
    region = ((total + REGION_SLACK + MOE_TM - 1) // MOE_TM) * MOE_TM
    start = jnp.cumsum(region) - region
    first = start[None, :] + jnp.cumsum(cnt, axis=0) - cnt
    a = jnp.concatenate([first, (start + total)[None, :]], axis=0).reshape(-1).astype(i32)
    ntile = (total + MOE_TM - 1) // MOE_TM
    tend = jnp.cumsum(ntile)
    n_used = tend[-1]
    max_tiles = 2 * n // MOE_TM + N_EXPERTS
    tile = jnp.minimum(jnp.arange(max_tiles, dtype=i32), n_used - 1)
    tile_expert = jnp.minimum(jnp.sum((tend[None, :] <= tile[:, None]).astype(i32), axis=1), N_EXPERTS - 1)
    tile_block = start[tile_expert] // MOE_TM + tile - (tend - ntile)[tile_expert]
    fill = jnp.max(jnp.bitwise_and(first, ROW_ALIGN - 1) + cnt, axis=1)
    fast_dispatch = jnp.logical_and(fill <= DISPATCH_FAST_FILL, jnp.arange(nt) < nt - 1).astype(i32)
    fast_combine = (fill <= COMBINE_FAST_WIN).astype(i32)
    return (a, tile_expert.astype(i32), tile_block.astype(i32), n_used.reshape(1).astype(i32),
            fast_dispatch, fast_combine)


def _sorted_rows(n):
    return -(-(2 * n + N_EXPERTS * (REGION_SLACK + MOE_TM)) // MOE_TM) * MOE_TM


def _window(a_ref, t, e):
    a = a_ref[t * N_EXPERTS + e]
    off = jnp.bitwise_and(a, ROW_ALIGN - 1)
    return pl.multiple_of(a - off, ROW_ALIGN), off


def _expert_slot(e, e1, e2, r1, r2, off):
    d = jnp.where(e1 == e, r1, jnp.where(e2 == e, r2, -1.0))
    return jnp.where(d >= 0.0, d + off.astype(F32), -1.0).astype(jnp.int32)


def _dispatch_kernel(a_ref, fast_ref, hn_ref, rt_ref, zeros_hbm, xs_hbm, carry, stage, sem):
    del zeros_hbm
    t = pl.program_id(0)
    nt = pl.num_programs(0)
    slot = t % 2

    def window_copy(sl, e, base, win):
        return pltpu.make_async_copy(stage.at[sl, e, 0:win], xs_hbm.at[pl.ds(base, win)], sem)

    @pl.when(t == 0)
    def _():
        carry[...] = jnp.zeros(carry.shape, F32)

    def stage_and_send(win, prev_win_is_fast):
        hn = hn_ref[...]
        rt = rt_ref[...]
        e1, e2, r1, r2 = rt[0:1], rt[1:2], rt[2:3], rt[3:4]
        row = lax.broadcasted_iota(jnp.int32, (win, MOE_TT), 0)
        head_row = lax.broadcasted_iota(jnp.int32, (ROW_ALIGN, 1), 0)
        bases = []
        for e in range(N_EXPERTS):
            base, off = _window(a_ref, t, e)
            bases.append(base)
            slot_of = _expert_slot(e, e1, e2, r1, r2, off)
            sel = jnp.where(row == slot_of, 1.0, 0.0).astype(BF16)
            rows = jnp.dot(sel, hn, preferred_element_type=F32)
            head = jnp.where(head_row < off, carry[e], rows[0:ROW_ALIGN])
            stage[slot, e, 0:ROW_ALIGN, :] = head.astype(BF16)
            stage[slot, e, ROW_ALIGN:win, :] = rows[ROW_ALIGN:].astype(BF16)
            filled = off + a_ref[(t + 1) * N_EXPERTS + e] - a_ref[t * N_EXPERTS + e]
            last_group = pl.multiple_of(lax.shift_right_logical(filled, 4) * ROW_ALIGN, ROW_ALIGN)
            carry[e] = stage[slot, e, pl.ds(last_group, ROW_ALIGN), :].astype(F32)

        for was_fast, prev_win in ((1, DISPATCH_FAST_WIN), (0, MOE_WIN)):
            @pl.when(jnp.logical_and(t > 0, prev_win_is_fast == was_fast))
            def _():
                for e in range(N_EXPERTS):
                    window_copy(1 - slot, e, 0, prev_win).wait()

        for e in range(N_EXPERTS):
            window_copy(slot, e, bases[e], win).start()

    is_fast = fast_ref[t]
    prev_fast = fast_ref[jnp.maximum(t - 1, 0)]

    @pl.when(is_fast == 1)
    def _():
        stage_and_send(DISPATCH_FAST_WIN, prev_fast)

    @pl.when(is_fast == 0)
    def _():
        stage_and_send(MOE_WIN, prev_fast)

    @pl.when(t == nt - 1)
    def _():
        for e in range(N_EXPERTS):
            window_copy(slot, e, 0, MOE_WIN).wait()


def _dispatch(hn, route_t, a, fast, zeros):
    n = hn.shape[0]
    tt = MOE_TT
    grid_spec = pltpu.PrefetchScalarGridSpec(
        num_scalar_prefetch=2,
        grid=(n // tt,),
        in_specs=[
            pl.BlockSpec((tt, D_MODEL), lambda t, a, fs: (t, 0)),
            pl.BlockSpec((ROUTE_ROWS, tt), lambda t, a, fs: (0, t)),
            pl.BlockSpec(memory_space=pl.ANY),
        ],
        out_specs=pl.BlockSpec(memory_space=pl.ANY),
        scratch_shapes=[
            pltpu.VMEM((N_EXPERTS, ROW_ALIGN, D_MODEL), F32),
            pltpu.VMEM((2, N_EXPERTS, MOE_WIN, D_MODEL), BF16),
            pltpu.SemaphoreType.DMA(()),
        ],
    )
    return pl.pallas_call(
        _dispatch_kernel,
        grid_spec=grid_spec,
        out_shape=jax.ShapeDtypeStruct(zeros.shape, BF16),
        input_output_aliases={4: 0},
        compiler_params=pltpu.CompilerParams(
            dimension_semantics=("arbitrary",), vmem_limit_bytes=VMEM_LIMIT_BYTES),
        name="moe_dispatch",
    )(a, fast, hn, route_t, zeros)


def _combine_kernel(a_ref, fast_ref, x_ref, rc_ref, ys_hbm, o_ref, ybuf, yfast, sem):
    t = pl.program_id(0)
    nt = pl.num_programs(0)
    slot = t % 2
    x = x_ref[...]
    rc = rc_ref[...]
    e1, e2, r1, r2, g1, g2 = (rc[:, k:k + 1] for k in range(6))

    def wide_copy(tile, sl, e):
        base, _ = _window(a_ref, tile, e)
        return pltpu.make_async_copy(ys_hbm.at[pl.ds(base, MOE_WIN)], ybuf.at[sl, e], sem.at[sl])

    def narrow_copy(tile, sl, e):
        base, _ = _window(a_ref, tile, e)
        return pltpu.make_async_copy(ys_hbm.at[pl.ds(base, COMBINE_FAST_WIN)],
                                     yfast.at[sl, pl.ds(e * COMBINE_FAST_WIN, COMBINE_FAST_WIN)], sem.at[sl])

    def start_windows(tile, sl):
        for flag, copy in ((1, narrow_copy), (0, wide_copy)):
            @pl.when(fast_ref[tile] == flag)
            def _():
                for e in range(N_EXPERTS):
                    copy(tile, sl, e).start()

    @pl.when(t == 0)
    def _():
        start_windows(0, 0)

    @pl.when(t + 1 < nt)
    def _():
        start_windows(t + 1, 1 - slot)

    @pl.when(fast_ref[t] == 0)
    def _():
        for e in range(N_EXPERTS):
            wide_copy(t, slot, e).wait()
        lane = lax.broadcasted_iota(jnp.int32, (MOE_TT, MOE_WIN), 1)
        acc = jnp.zeros((MOE_TT, D_MODEL), F32)
        for e in range(N_EXPERTS):
            _, off = _window(a_ref, t, e)
            slot_of = _expert_slot(e, e1, e2, r1, r2, off)
            sel = jnp.where(lane == slot_of, 1.0, 0.0).astype(BF16)
            gate = jnp.where(e1 == e, g1, jnp.where(e2 == e, g2, 0.0))
            acc = acc + gate * jnp.dot(sel, ybuf[slot, e], preferred_element_type=F32)
        o_ref[...] = x + acc

    @pl.when(fast_ref[t] == 1)
    def _():
        for e in range(N_EXPERTS):
            narrow_copy(t, slot, e).wait()
        shape = (MOE_TT, COMBINE_FAST_WIN)
        lane = lax.broadcasted_iota(jnp.int32, shape, 1).astype(F32)
        offs = [_window(a_ref, t, e)[1].astype(F32) for e in range(N_EXPERTS)]
        picked = []
        for ek, rk in ((e1, r1), (e2, r2)):
            eb = jnp.broadcast_to(ek, shape)
            pos = jnp.broadcast_to(rk, shape)
            for e in range(N_EXPERTS):
                pos = pos + jnp.where(eb == e, offs[e], 0.0)
            hit = lane == pos
            sel = jnp.concatenate(
                [jnp.where(jnp.logical_and(eb == e, hit), 1.0, 0.0).astype(BF16) for e in range(N_EXPERTS)],
                axis=1)
            picked.append(jnp.dot(sel, yfast[slot], preferred_element_type=F32))
        o_ref[...] = x + (g1 * picked[0] + g2 * picked[1])


def _combine(x2d, route_c, ys, a, fast):
    n = x2d.shape[0]
    tt = MOE_TT
    grid_spec = pltpu.PrefetchScalarGridSpec(
        num_scalar_prefetch=2,
        grid=(n // tt,),
        in_specs=[
            pl.BlockSpec((tt, D_MODEL), lambda t, a, fs: (t, 0)),
            pl.BlockSpec((tt, ROUTER_PAD), lambda t, a, fs: (t, 0)),
            pl.BlockSpec(memory_space=pl.ANY),
        ],
        out_specs=pl.BlockSpec((tt, D_MODEL), lambda t, a, fs: (t, 0)),
        scratch_shapes=[
            pltpu.VMEM((2, N_EXPERTS, MOE_WIN, D_MODEL), BF16),
            pltpu.VMEM((2, N_EXPERTS * COMBINE_FAST_WIN, D_MODEL), BF16),
            pltpu.SemaphoreType.DMA((2,)),
        ],
    )
    return pl.pallas_call(
        _combine_kernel,
        grid_spec=grid_spec,
        out_shape=jax.ShapeDtypeStruct((n, D_MODEL), F32),
        compiler_params=pltpu.CompilerParams(
            dimension_semantics=("arbitrary",), vmem_limit_bytes=VMEM_LIMIT_BYTES),
        name="moe_combine",
    )(a, fast, x2d, route_c, ys)


def _block_diag_ones(width):
    r = jnp.arange(width) // HEAD_DIM
    return (r[:, None] == r[None, :]).astype(BF16)


def _swa_bias():
    qi = jnp.arange(PAIR)[:, None]
    kj = jnp.arange(WIN)[None, :]
    dist = jnp.abs(qi + HALO - kj).astype(F32)
    kc = kj // CHUNK
    qc = qi // CHUNK
    visible = jnp.logical_and(kc >= qc, kc <= qc + 2)
    slopes = jnp.asarray([2.0 ** (-8.0 * (i + 1) / SWA_HEADS) for i in range(SWA_HEADS)], F32)
    bias = -slopes[:, None, None] * dist[None]
    return jnp.where(visible[None], bias, -jnp.inf)


def kernel(x, mem, g_mix, w_in, g_q_swa, g_k_swa, sinks, conv_w, g_mem, w_mem_kv, g_q_mem, g_k_mem,
           g_out_swa, g_out_conv, g_out_mem, w_out, g_ffn, w_gate_dense, w_up_dense, w_down_dense,
           w_router, b_router, w_gate_moe, w_up_moe, w_down_moe):
    batch, seq, _ = x.shape
    depth = g_mix.shape[0]
    n = batch * seq
    scale = HEAD_DIM ** -0.5

    bd_qk = _block_diag_ones(QK_WIDTH)
    bd_mem = _block_diag_ones(MEM_WIDTH)
    bias = _swa_bias()
    gqk = jnp.concatenate([jnp.tile(g_q_swa * scale, (1, SWA_HEADS)),
                           jnp.tile(g_k_swa, (1, SWA_KV_HEADS))], axis=1)[:, None, :]
    gqm = jnp.tile(g_q_mem * scale, (1, MEM_HEADS))[:, None, :]
    gkm = jnp.tile(g_k_mem, (1, MEM_HEADS))[:, None, :]
    gout = jnp.concatenate([g_out_swa, g_out_conv, g_out_mem], axis=1)[:, None, :]

    w_in_b = w_in.astype(BF16)
    w_out_b = w_out.astype(BF16)
    mem_kv_all = _mem_kv(mem.reshape(-1, D_MODEL), g_mem[:, None, :], w_mem_kv.astype(BF16), gkm, bd_mem)

    wr = jnp.pad(w_router, ((0, 0), (0, 0), (0, ROUTER_PAD - N_EXPERTS)))
    wr_hi = wr.astype(BF16)
    wr_lo = (wr - wr_hi.astype(F32)).astype(BF16)
    br = jnp.pad(b_router, ((0, 0), (0, ROUTER_PAD - N_EXPERTS)), constant_values=-jnp.inf)[:, None, :]
    ltri = (jnp.arange(MOE_TT)[:, None] >= jnp.arange(MOE_TT)[None, :]).astype(BF16)

    wg_dense, wu_dense, wd_dense = (w.astype(BF16) for w in (w_gate_dense, w_up_dense, w_down_dense))
    wg_moe, wu_moe, wd_moe = (w.astype(BF16) for w in (w_gate_moe, w_up_moe, w_down_moe))

    xs = x.reshape(n, D_MODEL)
    for l in range(depth):
        xs = _mixer(xs, batch, seq, sinks[l], g_mix[l][None], w_in_b[l], gqk[l], gqm[l], bd_qk, bd_mem,
                    bias, conv_w[l], [t[l] for t in mem_kv_all], gout[l], w_out_b[l])
        i = l // 2
        gf = g_ffn[l][None]
        if l % 2 == 0:
            xs = _dense_ffn(xs, gf, i, wg_dense, wu_dense, wd_dense)
        else:
            hn, route_c, route_t, cnt, xs_zeros, ys_zeros = _router(xs, gf, wr_hi[i], wr_lo[i], br[i], ltri)
            a, tile_expert, tile_block, n_used, fast_d, fast_c = _routing_tables(cnt, n)
            x_sorted = _dispatch(hn, route_t, a, fast_d, xs_zeros)
            y_sorted = _moe_ffn(x_sorted, tile_expert, tile_block, n_used, i, wg_moe, wu_moe, wd_moe, ys_zeros)
            xs = _combine(xs, route_c, y_sorted, a, fast_c)
    return xs.reshape(batch, seq, D_MODEL)
```

```python
import functools

import jax
import jax.numpy as jnp
from jax import lax
from jax.experimental import pallas as pl
from jax.experimental.pallas import tpu as pltpu

F32 = jnp.float32
BF16 = jnp.bfloat16

D_MODEL = 1024
CHUNK = 64
HEAD_DIM = 64
SWA_HEADS = 8
SWA_KV_HEADS = 2
SWA_GROUP = SWA_HEADS // SWA_KV_HEADS
CONV_WIDTH = 256
CONV_K = 3
MEM_HEADS = 4
SWA_WIDTH = SWA_HEADS * HEAD_DIM
KV_WIDTH = SWA_KV_HEADS * HEAD_DIM
MEM_WIDTH = MEM_HEADS * HEAD_DIM
QK_WIDTH = SWA_WIDTH + KV_WIDTH
IN_WIDTH = SWA_WIDTH + 2 * KV_WIDTH + 3 * CONV_WIDTH + MEM_WIDTH
D_FF = 3584
N_EXPERTS = 8
EPS = 1e-6

PAIR = 2 * CHUNK
WIN = 4 * CHUNK
HALO = WIN - PAIR
CONV_HALO = 8
LANES = 128
ROUTER_PAD = LANES

VMEM_LIMIT_BYTES = 56 * 1024 * 1024

NT_DIMS = (((1,), (1,)), ((), ()))


def _rms_scale(x, width):
    return lax.rsqrt(jnp.sum(x * x, axis=-1, keepdims=True) * (1.0 / width) + EPS)


def _head_rms(t, bd_ref):
    ss = jnp.dot((t * t).astype(BF16), bd_ref[...], preferred_element_type=F32)
    return lax.rsqrt(ss * (1.0 / HEAD_DIM) + EPS)


def _low_half(shape):
    return lax.broadcasted_iota(jnp.int32, shape, len(shape) - 1) % LANES < HEAD_DIM


def _softmax_rows(s, sink=None):
    m = jnp.max(s, axis=-1, keepdims=True)
    if sink is not None:
        m = jnp.maximum(m, sink)
    p = jnp.exp(s - m)
    den = jnp.sum(p, axis=-1, keepdims=True)
    if sink is not None:
        den = den + jnp.exp(sink - m)
    return (p * (1.0 / den)).astype(BF16)


def _memkv_kernel(mem_ref, g_ref, w_ref, gk_ref, bd_ref, mka_ref, mkb_ref, mva_ref, mvb_ref):
    m = mem_ref[...]
    hm = (m * _rms_scale(m, D_MODEL) * g_ref[...]).astype(BF16)
    kv = jnp.dot(hm, w_ref[...], preferred_element_type=F32)
    k = kv[:, :MEM_WIDTH]
    k = k * _head_rms(k, bd_ref) * gk_ref[...]
    v = kv[:, MEM_WIDTH:]
    low = _low_half(k.shape)
    mka_ref[...] = jnp.where(low, k, 0.0).astype(BF16)
    mkb_ref[...] = jnp.where(low, 0.0, k).astype(BF16)
    mva_ref[...] = jnp.where(low, v, 0.0).astype(BF16)
    mvb_ref[...] = jnp.where(low, 0.0, v).astype(BF16)


def _mem_kv(mem2d, g_mem, w_mem_kv, gk_mem, bd_mem):
    depth = g_mem.shape[0]
    rows = mem2d.shape[0]
    tr = min(rows, 512)
    out = jax.ShapeDtypeStruct((depth, rows, MEM_WIDTH), BF16)
    out_spec = pl.BlockSpec((None, tr, MEM_WIDTH), lambda l, i: (l, i, 0))
    return pl.pallas_call(
        _memkv_kernel,
        grid=(depth, rows // tr),
        in_specs=[
            pl.BlockSpec((tr, D_MODEL), lambda l, i: (i, 0)),
            pl.BlockSpec((None, 1, D_MODEL), lambda l, i: (l, 0, 0)),
            pl.BlockSpec((None, D_MODEL, 2 * MEM_WIDTH), lambda l, i: (l, 0, 0)),
            pl.BlockSpec((None, 1, MEM_WIDTH), lambda l, i: (l, 0, 0)),
            pl.BlockSpec((MEM_WIDTH, MEM_WIDTH), lambda l, i: (0, 0)),
        ],
        out_specs=[out_spec] * 4,
        out_shape=[out] * 4,
        compiler_params=pltpu.CompilerParams(
            dimension_semantics=("arbitrary", "arbitrary"), vmem_limit_bytes=VMEM_LIMIT_BYTES),
        name="mem_kv",
    )(mem2d, g_mem, w_mem_kv, gk_mem, bd_mem)


MEM_ROWS = 256


def _mixer_kernel(sinks_ref, x_ref, gmix_ref, win_ref, gqk_ref, gqm_ref, bdqk_ref, bdm_ref, bias_ref,
                  convw_ref, mka_ref, mkb_ref, mva_ref, mvb_ref, gout_ref, wout_ref, o_ref,
                  kbuf, vbuf, zbuf, yswa, ymem, sbuf, pbuf, smem, pmem, *, ts):
    s_idx = pl.program_id(1)
    x = x_ref[...]
    h = (x * _rms_scale(x, D_MODEL) * gmix_ref[...]).astype(BF16)
    proj = jnp.dot(h, win_ref[...], preferred_element_type=F32)

    @pl.when(s_idx == 0)
    def _():
        kbuf[:, 0:HALO, :] = jnp.zeros((2 * SWA_KV_HEADS, HALO, LANES), BF16)
        vbuf[:, 0:HALO, :] = jnp.zeros((2 * SWA_KV_HEADS, HALO, LANES), BF16)
        zbuf[0:CONV_HALO, :] = jnp.zeros((CONV_HALO, CONV_WIDTH), F32)

    qk = proj[:, :QK_WIDTH]
    qk = qk * _head_rms(qk, bdqk_ref) * gqk_ref[...]
    q = qk[:, :SWA_WIDTH].astype(BF16)
    low = _low_half((ts, KV_WIDTH))
    for buf, t in ((kbuf, qk[:, SWA_WIDTH:]), (vbuf, proj[:, QK_WIDTH:QK_WIDTH + KV_WIDTH])):
        swapped = pltpu.roll(t, HEAD_DIM, 1)
        buf[0, HALO:HALO + ts, :] = jnp.where(low, t, 0.0).astype(BF16)
        buf[1, HALO:HALO + ts, :] = jnp.where(low, 0.0, swapped).astype(BF16)
        buf[2, HALO:HALO + ts, :] = jnp.where(low, swapped, 0.0).astype(BF16)
        buf[3, HALO:HALO + ts, :] = jnp.where(low, 0.0, t).astype(BF16)

    key_lane = lax.broadcasted_iota(jnp.int32, (1, WIN), 1)
    first_keys_valid = jnp.logical_or(key_lane >= HALO, s_idx > 0)

    blocks = [(j, kh, half) for j in range(ts // PAIR) for kh in range(SWA_KV_HEADS) for half in range(2)]
    for n_blk, (j, kh, half) in enumerate(blocks):
        r0 = j * PAIR
        c0 = kh * SWA_GROUP * HEAD_DIM
        qg = jnp.concatenate([q[r0:r0 + PAIR, c0:c0 + LANES],
                              q[r0:r0 + PAIR, c0 + LANES:c0 + 2 * LANES]], axis=0)
        sbuf[n_blk] = lax.dot_general(qg, kbuf[2 * kh + half, r0:r0 + WIN, :], NT_DIMS,
                                      preferred_element_type=F32)
    for n_blk, (j, kh, half) in enumerate(blocks):
        for pr in range(2):
            hd = kh * SWA_GROUP + 2 * pr + half
            sg = sbuf[n_blk, pr * PAIR:(pr + 1) * PAIR, :] + bias_ref[hd]
            if j == 0:
                sg = jnp.where(first_keys_valid, sg, -jnp.inf)
            pbuf[n_blk, pr * PAIR:(pr + 1) * PAIR, :] = _softmax_rows(sg, sinks_ref[hd])
    for j in range(ts // PAIR):
        r0 = j * PAIR
        for kh in range(SWA_KV_HEADS):
            c0 = kh * SWA_GROUP * HEAD_DIM
            n_blk = (j * SWA_KV_HEADS + kh) * 2
            o = (jnp.dot(pbuf[n_blk], vbuf[2 * kh, r0:r0 + WIN, :], preferred_element_type=F32)
                 + jnp.dot(pbuf[n_blk + 1], vbuf[2 * kh + 1, r0:r0 + WIN, :], preferred_element_type=F32))
            yswa[r0:r0 + PAIR, c0:c0 + LANES] = o[0:PAIR]
            yswa[r0:r0 + PAIR, c0 + LANES:c0 + 2 * LANES] = o[PAIR:]

    kbuf[:, 0:HALO, :] = kbuf[:, ts:ts + HALO, :]
    vbuf[:, 0:HALO, :] = vbuf[:, ts:ts + HALO, :]

    c0 = QK_WIDTH + KV_WIDTH
    gate_b = proj[:, c0:c0 + CONV_WIDTH]
    z = proj[:, c0 + CONV_WIDTH:c0 + 2 * CONV_WIDTH] * proj[:, c0 + 2 * CONV_WIDTH:c0 + 3 * CONV_WIDTH]
    zbuf[CONV_HALO:CONV_HALO + ts, :] = z
    z1 = zbuf[CONV_HALO - 1:CONV_HALO - 1 + ts, :]
    z2 = zbuf[CONV_HALO - 2:CONV_HALO - 2 + ts, :]
    cw = convw_ref[...]
    y_conv = gate_b * (cw[0:1] * z2 + cw[1:2] * z1 + cw[2:3] * z)
    zbuf[0:CONV_HALO, :] = zbuf[ts:ts + CONV_HALO, :]

    qm = proj[:, IN_WIDTH - MEM_WIDTH:]
    qm = (qm * _head_rms(qm, bdm_ref) * gqm_ref[...]).astype(BF16)
    mem_rows = min(ts, MEM_ROWS)
    mem_blocks = [(pr, rb) for pr in range(MEM_HEADS // 2) for rb in range(ts // mem_rows)]
    for n_blk, (pr, rb) in enumerate(mem_blocks):
        qp = qm[rb * mem_rows:(rb + 1) * mem_rows, pr * LANES:(pr + 1) * LANES]
        for half, mk_ref in enumerate((mka_ref, mkb_ref)):
            smem[2 * n_blk + half] = lax.dot_general(qp, mk_ref[:, pr * LANES:(pr + 1) * LANES], NT_DIMS,
                                                     preferred_element_type=F32)
    for n_blk in range(2 * len(mem_blocks)):
        pmem[n_blk] = _softmax_rows(smem[n_blk])
    for n_blk, (pr, rb) in enumerate(mem_blocks):
        cols = slice(pr * LANES, (pr + 1) * LANES)
        ymem[rb * mem_rows:(rb + 1) * mem_rows, cols] = (
            jnp.dot(pmem[2 * n_blk], mva_ref[:, cols], preferred_element_type=F32)
            + jnp.dot(pmem[2 * n_blk + 1], mvb_ref[:, cols], preferred_element_type=F32))

    gout = gout_ref[...]
    ys = yswa[...]
    ym = ymem[...]
    a = (ys * _rms_scale(ys, SWA_WIDTH) * gout[:, :SWA_WIDTH]).astype(BF16)
    b = (y_conv * _rms_scale(y_conv, CONV_WIDTH) * gout[:, SWA_WIDTH:SWA_WIDTH + CONV_WIDTH]).astype(BF16)
    c = (ym * _rms_scale(ym, MEM_WIDTH) * gout[:, SWA_WIDTH + CONV_WIDTH:]).astype(BF16)
    out = x + jnp.dot(a, wout_ref[0:SWA_WIDTH, :], preferred_element_type=F32)
    out = out + jnp.dot(b, wout_ref[SWA_WIDTH:SWA_WIDTH + CONV_WIDTH, :], preferred_element_type=F32)
    out = out + jnp.dot(c, wout_ref[SWA_WIDTH + CONV_WIDTH:, :], preferred_element_type=F32)
    o_ref[...] = out


def _mixer(x2d, batch, seq, sinks, gmix, w_in, gqk, gqm, bd_qk, bd_mem, bias, conv_w, mem_kv, gout, w_out):
    ts = min(seq, 1024)
    nseq = seq // ts
    mem_len = mem_kv[0].shape[0] // batch
    mem_rows = min(ts, MEM_ROWS)
    const = lambda b, s, sk: (0, 0)
    mem_spec = pl.BlockSpec((mem_len, MEM_WIDTH), lambda b, s, sk: (b, 0))
    grid_spec = pltpu.PrefetchScalarGridSpec(
        num_scalar_prefetch=1,
        grid=(batch, nseq),
        in_specs=[
            pl.BlockSpec((ts, D_MODEL), lambda b, s, sk: (b * nseq + s, 0)),
            pl.BlockSpec((1, D_MODEL), const),
            pl.BlockSpec((D_MODEL, IN_WIDTH), const),
            pl.BlockSpec((1, QK_WIDTH), const),
            pl.BlockSpec((1, MEM_WIDTH), const),
            pl.BlockSpec((QK_WIDTH, QK_WIDTH), const),
            pl.BlockSpec((MEM_WIDTH, MEM_WIDTH), const),
            pl.BlockSpec((SWA_HEADS, PAIR, WIN), lambda b, s, sk: (0, 0, 0)),
            pl.BlockSpec((CONV_K, CONV_WIDTH), const),
            mem_spec, mem_spec, mem_spec, mem_spec,
            pl.BlockSpec((1, D_MODEL), const),
            pl.BlockSpec((D_MODEL, D_MODEL), const),
        ],
        out_specs=pl.BlockSpec((ts, D_MODEL), lambda b, s, sk: (b * nseq + s, 0)),
        scratch_shapes=[
            pltpu.VMEM((2 * SWA_KV_HEADS, ts + HALO, LANES), BF16),
            pltpu.VMEM((2 * SWA_KV_HEADS, ts + HALO, LANES), BF16),
            pltpu.VMEM((ts + CONV_HALO, CONV_WIDTH), F32),
            pltpu.VMEM((ts, SWA_WIDTH), F32),
            pltpu.VMEM((ts, MEM_WIDTH), F32),
            pltpu.VMEM((ts // PAIR * 2 * SWA_KV_HEADS, 2 * PAIR, WIN), F32),
            pltpu.VMEM((ts // PAIR * 2 * SWA_KV_HEADS, 2 * PAIR, WIN), BF16),
            pltpu.VMEM((MEM_HEADS * (ts // mem_rows), mem_rows, mem_len), F32),
            pltpu.VMEM((MEM_HEADS * (ts // mem_rows), mem_rows, mem_len), BF16),
        ],
    )
    return pl.pallas_call(
        functools.partial(_mixer_kernel, ts=ts),
        grid_spec=grid_spec,
        out_shape=jax.ShapeDtypeStruct(x2d.shape, F32),
        compiler_params=pltpu.CompilerParams(
            dimension_semantics=("arbitrary", "arbitrary"), vmem_limit_bytes=VMEM_LIMIT_BYTES),
        name="token_mixer",
    )(sinks, x2d, gmix, w_in, gqk, gqm, bd_qk, bd_mem, bias, conv_w, *mem_kv, gout, w_out)


FF_CHUNK = 256


def _swiglu_accumulate(h, wg_ref, wu_ref, wd_ref, acc_ref):
    tf = wg_ref.shape[-1]
    for c in range(tf // FF_CHUNK):
        sl = slice(c * FF_CHUNK, (c + 1) * FF_CHUNK)
        gate = jnp.dot(h, wg_ref[:, sl], preferred_element_type=F32)
        up = jnp.dot(h, wu_ref[:, sl], preferred_element_type=F32)
        act = (gate * jax.nn.sigmoid(gate) * up).astype(BF16)
        acc_ref[...] += jnp.dot(act, wd_ref[sl, :], preferred_element_type=F32)


def _dense_ffn_kernel(x_ref, g_ref, wg_ref, wu_ref, wd_ref, o_ref, h_ref):
    @pl.when(pl.program_id(1) == 0)
    def _():
        x = x_ref[...]
        h_ref[...] = (x * _rms_scale(x, D_MODEL) * g_ref[...]).astype(BF16)
        o_ref[...] = x

    _swiglu_accumulate(h_ref[...], wg_ref, wu_ref, wd_ref, o_ref)


def _dense_ffn(x2d, g, layer, wg, wu, wd):
    n = x2d.shape[0]
    tm = min(n, 512)
    tf = D_FF
    return pl.pallas_call(
        _dense_ffn_kernel,
        grid=(n // tm, D_FF // tf),
        in_specs=[
            pl.BlockSpec((tm, D_MODEL), lambda i, f: (i, 0)),
            pl.BlockSpec((1, D_MODEL), lambda i, f: (0, 0)),
            pl.BlockSpec((None, D_MODEL, tf), lambda i, f: (layer, 0, f)),
            pl.BlockSpec((None, D_MODEL, tf), lambda i, f: (layer, 0, f)),
            pl.BlockSpec((None, tf, D_MODEL), lambda i, f: (layer, f, 0)),
        ],
        out_specs=pl.BlockSpec((tm, D_MODEL), lambda i, f: (i, 0)),
        out_shape=jax.ShapeDtypeStruct(x2d.shape, F32),
        scratch_shapes=[pltpu.VMEM((tm, D_MODEL), BF16)],
        compiler_params=pltpu.CompilerParams(
            dimension_semantics=("arbitrary", "arbitrary"), vmem_limit_bytes=VMEM_LIMIT_BYTES),
        name="dense_ffn",
    )(x2d, g, wg, wu, wd)


MOE_TT = 256
MOE_TM = 512
ROW_ALIGN = 16
MOE_WIN = MOE_TT + ROW_ALIGN
DISPATCH_FAST_WIN = 144
DISPATCH_FAST_FILL = DISPATCH_FAST_WIN - ROW_ALIGN
COMBINE_FAST_WIN = LANES
REGION_SLACK = MOE_WIN
ROUTE_ROWS = 8
ROUTER_ROWS = 1024


def _moe_ffn_kernel(te_ref, tb_ref, nu_ref, x_ref, wg_ref, wu_ref, wd_ref, ys_hbm, o_ref, acc_ref):
    del te_ref, tb_ref, ys_hbm
    f = pl.program_id(1)
    used = pl.program_id(0) < nu_ref[0]

    @pl.when(jnp.logical_and(used, f == 0))
    def _():
        acc_ref[...] = jnp.zeros(acc_ref.shape, F32)

    @pl.when(used)
    def _():
        _swiglu_accumulate(x_ref[...], wg_ref, wu_ref, wd_ref, acc_ref)

    @pl.when(jnp.logical_and(used, f == pl.num_programs(1) - 1))
    def _():
        o_ref[...] = acc_ref[...].astype(BF16)


def _moe_ffn(xs, tile_expert, tile_block, n_used, layer, wg, wu, wd, ys):
    p = xs.shape[0]
    tm = MOE_TM
    tf = D_FF
    nf = D_FF // tf
    max_tiles = tile_expert.shape[0]

    def f_eff(i, f, nu):
        return jnp.where(i < nu[0], f, nf - 1)

    grid_spec = pltpu.PrefetchScalarGridSpec(
        num_scalar_prefetch=3,
        grid=(max_tiles, nf),
        in_specs=[
            pl.BlockSpec((tm, D_MODEL), lambda i, f, te, tb, nu: (tb[i], 0)),
            pl.BlockSpec((None, None, D_MODEL, tf), lambda i, f, te, tb, nu: (layer, te[i], 0, f_eff(i, f, nu))),
            pl.BlockSpec((None, None, D_MODEL, tf), lambda i, f, te, tb, nu: (layer, te[i], 0, f_eff(i, f, nu))),
            pl.BlockSpec((None, None, tf, D_MODEL), lambda i, f, te, tb, nu: (layer, te[i], f_eff(i, f, nu), 0)),
            pl.BlockSpec(memory_space=pl.ANY),
        ],
        out_specs=pl.BlockSpec((tm, D_MODEL), lambda i, f, te, tb, nu: (tb[i], 0)),
        scratch_shapes=[pltpu.VMEM((tm, D_MODEL), F32)],
    )
    return pl.pallas_call(
        _moe_ffn_kernel,
        grid_spec=grid_spec,
        out_shape=jax.ShapeDtypeStruct((p, D_MODEL), BF16),
        input_output_aliases={7: 0},
        compiler_params=pltpu.CompilerParams(
            dimension_semantics=("arbitrary", "arbitrary"), vmem_limit_bytes=VMEM_LIMIT_BYTES),
        name="moe_ffn",
    )(tile_expert, tile_block, n_used, xs, wg, wu, wd, ys)


def _router_kernel(x_ref, g_ref, wh_ref, wl_ref, b_ref, ltri_ref, hn_ref, rc_ref, rt_ref, cnt_ref, xs0_ref, ys0_ref):
    xs0_ref[...] = jnp.zeros(xs0_ref.shape, BF16)
    ys0_ref[...] = jnp.zeros(ys0_ref.shape, BF16)
    x = x_ref[...]
    h = x * _rms_scale(x, D_MODEL) * g_ref[...]
    hh = h.astype(BF16)
    hn_ref[...] = hh
    hl = (h - hh.astype(F32)).astype(BF16)
    wh = wh_ref[...]
    logits = (jnp.dot(hh, wh, preferred_element_type=F32)
              + jnp.dot(hl, wh, preferred_element_type=F32)
              + jnp.dot(hh, wl_ref[...], preferred_element_type=F32)) + b_ref[...]
    lane = lax.broadcasted_iota(jnp.int32, (MOE_TT, ROUTER_PAD), 1)
    for blk in range(x.shape[0] // MOE_TT):
        rows = slice(blk * MOE_TT, (blk + 1) * MOE_TT)
        lg = logits[rows]
        v1 = jnp.max(lg, axis=-1, keepdims=True)
        i1 = jnp.min(jnp.where(lg == v1, lane, ROUTER_PAD), axis=-1, keepdims=True)
        rest = jnp.where(lane == i1, -jnp.inf, lg)
        v2 = jnp.max(rest, axis=-1, keepdims=True)
        i2 = jnp.min(jnp.where(rest == v2, lane, ROUTER_PAD), axis=-1, keepdims=True)
        e2 = jnp.exp(v2 - v1)
        den = 1.0 + e2
        chosen = jnp.logical_or(lane == i1, lane == i2)
        cum = jnp.dot(ltri_ref[...], jnp.where(chosen, 1.0, 0.0).astype(BF16), preferred_element_type=F32)
        r1 = jnp.sum(jnp.where(lane == i1, cum, 0.0), axis=-1, keepdims=True) - 1.0
        r2 = jnp.sum(jnp.where(lane == i2, cum, 0.0), axis=-1, keepdims=True) - 1.0
        rc = jnp.zeros(lg.shape, F32)
        for k, col in enumerate((i1.astype(F32), i2.astype(F32), r1, r2, 1.0 / den, e2 / den)):
            rc = jnp.where(lane == k, col, rc)
        rc_ref[rows, :] = rc
        rt_ref[:, rows] = rc.T[:ROUTE_ROWS, :]
        cnt_ref[blk] = cum[MOE_TT - 1:MOE_TT, :].astype(jnp.int32)


def _router(x2d, g, w_hi, w_lo, b_pad, ltri):
    n = x2d.shape[0]
    tr = min(n, ROUTER_ROWS)
    tiles = tr // MOE_TT
    sorted_rows = _sorted_rows(n)
    zero_rows = sorted_rows // (n // tr)
    return pl.pallas_call(
        _router_kernel,
        grid=(n // tr,),
        in_specs=[
            pl.BlockSpec((tr, D_MODEL), lambda i: (i, 0)),
            pl.BlockSpec((1, D_MODEL), lambda i: (0, 0)),
            pl.BlockSpec((D_MODEL, ROUTER_PAD), lambda i: (0, 0)),
            pl.BlockSpec((D_MODEL, ROUTER_PAD), lambda i: (0, 0)),
            pl.BlockSpec((1, ROUTER_PAD), lambda i: (0, 0)),
            pl.BlockSpec((MOE_TT, MOE_TT), lambda i: (0, 0)),
        ],
        out_specs=[
            pl.BlockSpec((tr, D_MODEL), lambda i: (i, 0)),
            pl.BlockSpec((tr, ROUTER_PAD), lambda i: (i, 0)),
            pl.BlockSpec((ROUTE_ROWS, tr), lambda i: (0, i)),
            pl.BlockSpec((tiles, 1, ROUTER_PAD), lambda i: (i, 0, 0)),
            pl.BlockSpec((zero_rows, D_MODEL), lambda i: (i, 0)),
            pl.BlockSpec((zero_rows, D_MODEL), lambda i: (i, 0)),
        ],
        out_shape=[jax.ShapeDtypeStruct((n, D_MODEL), BF16),
                   jax.ShapeDtypeStruct((n, ROUTER_PAD), F32),
                   jax.ShapeDtypeStruct((ROUTE_ROWS, n), F32),
                   jax.ShapeDtypeStruct((n // MOE_TT, 1, ROUTER_PAD), jnp.int32),
                   jax.ShapeDtypeStruct((sorted_rows, D_MODEL), BF16),
                   jax.ShapeDtypeStruct((sorted_rows, D_MODEL), BF16)],
        compiler_params=pltpu.CompilerParams(
            dimension_semantics=("arbitrary",), vmem_limit_bytes=VMEM_LIMIT_BYTES),
        name="router",
    )(x2d, g, w_hi, w_lo, b_pad, ltri)


def _routing_tables(cnt, n):
    i32 = jnp.int32
    nt = n // MOE_TT
    cnt = cnt.reshape(nt, ROUTER_PAD)[:, :N_EXPERTS]
    total = jnp.sum(cnt, axis=0)
    region = ((total + REGION_SLACK + MOE_TM - 1) // MOE_TM) * MOE_TM
    start = jnp.cumsum(region) - region
    first = start[None, :] + jnp.cumsum(cnt, axis=0) - cnt
    a = jnp.concatenate([first, (start + total)[None, :]], axis=0).reshape(-1).astype(i32)
    ntile = (total + MOE_TM - 1) // MOE_TM
    tend = jnp.cumsum(ntile)
    n_used = tend[-1]
    max_tiles = 2 * n // MOE_TM + N_EXPERTS
    tile = jnp.minimum(jnp.arange(max_tiles, dtype=i32), n_used - 1)
    tile_expert = jnp.minimum(jnp.sum((tend[None, :] <= tile[:, None]).astype(i32), axis=1), N_EXPERTS - 1)
    tile_block = start[tile_expert] // MOE_TM + tile - (tend - ntile)[tile_expert]
    fill = jnp.max(jnp.bitwise_and(first, ROW_ALIGN - 1) + cnt, axis=1)
    fast_dispatch = jnp.logical_and(fill <= DISPATCH_FAST_FILL, jnp.arange(nt) < nt - 1).astype(i32)
    fast_combine = (fill <= COMBINE_FAST_WIN).astype(i32)
    return (a, tile_expert.astype(i32), tile_block.astype(i32), n_used.reshape(1).astype(i32),
            fast_dispatch, fast_combine)


def _sorted_rows(n):
    return -(-(2 * n + N_EXPERTS * (REGION_SLACK + MOE_TM)) // MOE_TM) * MOE_TM


def _window(a_ref, t, e):
    a = a_ref[t * N_EXPERTS + e]
    off = jnp.bitwise_and(a, ROW_ALIGN - 1)
    return pl.multiple_of(a - off, ROW_ALIGN), off


def _expert_slot(e, e1, e2, r1, r2, off):
    d = jnp.where(e1 == e, r1, jnp.where(e2 == e, r2, -1.0))
    return jnp.where(d >= 0.0, d + off.astype(F32), -1.0).astype(jnp.int32)


def _dispatch_kernel(a_ref, fast_ref, hn_ref, rt_ref, zeros_hbm, xs_hbm, carry, stage, sem):
    del zeros_hbm
    t = pl.program_id(0)
    nt = pl.num_programs(0)
    slot = t % 2

    def window_copy(sl, e, base, win):
        return pltpu.make_async_copy(stage.at[sl, e, 0:win], xs_hbm.at[pl.ds(base, win)], sem)

    @pl.when(t == 0)
    def _():
        carry[...] = jnp.zeros(carry.shape, F32)

    def stage_and_send(win, prev_win_is_fast):
        hn = hn_ref[...]
        rt = rt_ref[...]
        e1, e2, r1, r2 = rt[0:1], rt[1:2], rt[2:3], rt[3:4]
        row = lax.broadcasted_iota(jnp.int32, (win, MOE_TT), 0)
        head_row = lax.broadcasted_iota(jnp.int32, (ROW_ALIGN, 1), 0)
        bases = []
        for e in range(N_EXPERTS):
            base, off = _window(a_ref, t, e)
            bases.append(base)
            slot_of = _expert_slot(e, e1, e2, r1, r2, off)
            sel = jnp.where(row == slot_of, 1.0, 0.0).astype(BF16)
            rows = jnp.dot(sel, hn, preferred_element_type=F32)
            head = jnp.where(head_row < off, carry[e], rows[0:ROW_ALIGN])
            stage[slot, e, 0:ROW_ALIGN, :] = head.astype(BF16)
            stage[slot, e, ROW_ALIGN:win, :] = rows[ROW_ALIGN:].astype(BF16)
            filled = off + a_ref[(t + 1) * N_EXPERTS + e] - a_ref[t * N_EXPERTS + e]
            last_group = pl.multiple_of(lax.shift_right_logical(filled, 4) * ROW_ALIGN, ROW_ALIGN)
            carry[e] = stage[slot, e, pl.ds(last_group, ROW_ALIGN), :].astype(F32)

        for was_fast, prev_win in ((1, DISPATCH_FAST_WIN), (0, MOE_WIN)):
            @pl.when(jnp.logical_and(t > 0, prev_win_is_fast == was_fast))
            def _():
                for e in range(N_EXPERTS):
                    window_copy(1 - slot, e, 0, prev_win).wait()

        for e in range(N_EXPERTS):
            window_copy(slot, e, bases[e], win).start()

    is_fast = fast_ref[t]
    prev_fast = fast_ref[jnp.maximum(t - 1, 0)]

    @pl.when(is_fast == 1)
    def _():
        stage_and_send(DISPATCH_FAST_WIN, prev_fast)

    @pl.when(is_fast == 0)
    def _():
        stage_and_send(MOE_WIN, prev_fast)

    @pl.when(t == nt - 1)
    def _():
        for e in range(N_EXPERTS):
            window_copy(slot, e, 0, MOE_WIN).wait()


def _dispatch(hn, route_t, a, fast, zeros):
    n = hn.shape[0]
    tt = MOE_TT
    grid_spec = pltpu.PrefetchScalarGridSpec(
        num_scalar_prefetch=2,
        grid=(n // tt,),
        in_specs=[
            pl.BlockSpec((tt, D_MODEL), lambda t, a, fs: (t, 0)),
            pl.BlockSpec((ROUTE_ROWS, tt), lambda t, a, fs: (0, t)),
            pl.BlockSpec(memory_space=pl.ANY),
        ],
        out_specs=pl.BlockSpec(memory_space=pl.ANY),
        scratch_shapes=[
            pltpu.VMEM((N_EXPERTS, ROW_ALIGN, D_MODEL), F32),
            pltpu.VMEM((2, N_EXPERTS, MOE_WIN, D_MODEL), BF16),
            pltpu.SemaphoreType.DMA(()),
        ],
    )
    return pl.pallas_call(
        _dispatch_kernel,
        grid_spec=grid_spec,
        out_shape=jax.ShapeDtypeStruct(zeros.shape, BF16),
        input_output_aliases={4: 0},
        compiler_params=pltpu.CompilerParams(
            dimension_semantics=("arbitrary",), vmem_limit_bytes=VMEM_LIMIT_BYTES),
        name="moe_dispatch",
    )(a, fast, hn, route_t, zeros)


def _combine_kernel(a_ref, fast_ref, x_ref, rc_ref, ys_hbm, o_ref, ybuf, yfast, sem):
    t = pl.program_id(0)
    nt = pl.num_programs(0)
    slot = t % 2
    x = x_ref[...]
    rc = rc_ref[...]
    e1, e2, r1, r2, g1, g2 = (rc[:, k:k + 1] for k in range(6))

    def wide_copy(tile, sl, e):
        base, _ = _window(a_ref, tile, e)
        return pltpu.make_async_copy(ys_hbm.at[pl.ds(base, MOE_WIN)], ybuf.at[sl, e], sem.at[sl])

    def narrow_copy(tile, sl, e):
        base, _ = _window(a_ref, tile, e)
        return pltpu.make_async_copy(ys_hbm.at[pl.ds(base, COMBINE_FAST_WIN)],
                                     yfast.at[sl, pl.ds(e * COMBINE_FAST_WIN, COMBINE_FAST_WIN)], sem.at[sl])

    def start_windows(tile, sl):
        for flag, copy in ((1, narrow_copy), (0, wide_copy)):
            @pl.when(fast_ref[tile] == flag)
            def _():
                for e in range(N_EXPERTS):
                    copy(tile, sl, e).start()

    @pl.when(t == 0)
    def _():
        start_windows(0, 0)

    @pl.when(t + 1 < nt)
    def _():
        start_windows(t + 1, 1 - slot)

    @pl.when(fast_ref[t] == 0)
    def _():
        for e in range(N_EXPERTS):
            wide_copy(t, slot, e).wait()
        lane = lax.broadcasted_iota(jnp.int32, (MOE_TT, MOE_WIN), 1)
        acc = jnp.zeros((MOE_TT, D_MODEL), F32)
        for e in range(N_EXPERTS):
            _, off = _window(a_ref, t, e)
            slot_of = _expert_slot(e, e1, e2, r1, r2, off)
            sel = jnp.where(lane == slot_of, 1.0, 0.0).astype(BF16)
            gate = jnp.where(e1 == e, g1, jnp.where(e2 == e, g2, 0.0))
            acc = acc + gate * jnp.dot(sel, ybuf[slot, e], preferred_element_type=F32)
        o_ref[...] = x + acc

    @pl.when(fast_ref[t] == 1)
    def _():
        for e in range(N_EXPERTS):
            narrow_copy(t, slot, e).wait()
        shape = (MOE_TT, COMBINE_FAST_WIN)
        lane = lax.broadcasted_iota(jnp.int32, shape, 1).astype(F32)
        offs = [_window(a_ref, t, e)[1].astype(F32) for e in range(N_EXPERTS)]
        picked = []
        for ek, rk in ((e1, r1), (e2, r2)):
            eb = jnp.broadcast_to(ek, shape)
            pos = jnp.broadcast_to(rk, shape)
            for e in range(N_EXPERTS):
                pos = pos + jnp.where(eb == e, offs[e], 0.0)
            hit = lane == pos
            sel = jnp.concatenate(
                [jnp.where(jnp.logical_and(eb == e, hit), 1.0, 0.0).astype(BF16) for e in range(N_EXPERTS)],
                axis=1)
            picked.append(jnp.dot(sel, yfast[slot], preferred_element_type=F32))
        o_ref[...] = x + (g1 * picked[0] + g2 * picked[1])


def _combine(x2d, route_c, ys, a, fast):
    n = x2d.shape[0]
    tt = MOE_TT
    grid_spec = pltpu.PrefetchScalarGridSpec(
        num_scalar_prefetch=2,
        grid=(n // tt,),
        in_specs=[
            pl.BlockSpec((tt, D_MODEL), lambda t, a, fs: (t, 0)),
            pl.BlockSpec((tt, ROUTER_PAD), lambda t, a, fs: (t, 0)),
            pl.BlockSpec(memory_space=pl.ANY),
        ],
        out_specs=pl.BlockSpec((tt, D_MODEL), lambda t, a, fs: (t, 0)),
        scratch_shapes=[
            pltpu.VMEM((2, N_EXPERTS, MOE_WIN, D_MODEL), BF16),
            pltpu.VMEM((2, N_EXPERTS * COMBINE_FAST_WIN, D_MODEL), BF16),
            pltpu.SemaphoreType.DMA((2,)),
        ],
    )
    return pl.pallas_call(
        _combine_kernel,
        grid_spec=grid_spec,
        out_shape=jax.ShapeDtypeStruct((n, D_MODEL), F32),
        compiler_params=pltpu.CompilerParams(
            dimension_semantics=("arbitrary",), vmem_limit_bytes=VMEM_LIMIT_BYTES),
        name="moe_combine",
    )(a, fast, x2d, route_c, ys)


def _block_diag_ones(width):
    r = jnp.arange(width) // HEAD_DIM
    return (r[:, None] == r[None, :]).astype(BF16)


def _swa_bias():
    qi = jnp.arange(PAIR)[:, None]
    kj = jnp.arange(WIN)[None, :]
    dist = jnp.abs(qi + HALO - kj).astype(F32)
    kc = kj // CHUNK
    qc = qi // CHUNK
    visible = jnp.logical_and(kc >= qc, kc <= qc + 2)
    slopes = jnp.asarray([2.0 ** (-8.0 * (i + 1) / SWA_HEADS) for i in range(SWA_HEADS)], F32)
    bias = -slopes[:, None, None] * dist[None]
    return jnp.where(visible[None], bias, -jnp.inf)


def kernel(x, mem, g_mix, w_in, g_q_swa, g_k_swa, sinks, conv_w, g_mem, w_mem_kv, g_q_mem, g_k_mem,
           g_out_swa, g_out_conv, g_out_mem, w_out, g_ffn, w_gate_dense, w_up_dense, w_down_dense,
           w_router, b_router, w_gate_moe, w_up_moe, w_down_moe):
    batch, seq, _ = x.shape
    depth = g_mix.shape[0]
    n = batch * seq
    scale = HEAD_DIM ** -0.5

    bd_qk = _block_diag_ones(QK_WIDTH)
    bd_mem = _block_diag_ones(MEM_WIDTH)
    bias = _swa_bias()
    gqk = jnp.concatenate([jnp.tile(g_q_swa * scale, (1, SWA_HEADS)),
                           jnp.tile(g_k_swa, (1, SWA_KV_HEADS))], axis=1)[:, None, :]
    gqm = jnp.tile(g_q_mem * scale, (1, MEM_HEADS))[:, None, :]
    gkm = jnp.tile(g_k_mem, (1, MEM_HEADS))[:, None, :]
    gout = jnp.concatenate([g_out_swa, g_out_conv, g_out_mem], axis=1)[:, None, :]

    w_in_b = w_in.astype(BF16)
    w_out_b = w_out.astype(BF16)
    mem_kv_all = _mem_kv(mem.reshape(-1, D_MODEL), g_mem[:, None, :], w_mem_kv.astype(BF16), gkm, bd_mem)

    wr = jnp.pad(w_router, ((0, 0), (0, 0), (0, ROUTER_PAD - N_EXPERTS)))
    wr_hi = wr.astype(BF16)
    wr_lo = (wr - wr_hi.astype(F32)).astype(BF16)
    br = jnp.pad(b_router, ((0, 0), (0, ROUTER_PAD - N_EXPERTS)), constant_values=-jnp.inf)[:, None, :]
    ltri = (jnp.arange(MOE_TT)[:, None] >= jnp.arange(MOE_TT)[None, :]).astype(BF16)

    wg_dense, wu_dense, wd_dense = (w.astype(BF16) for w in (w_gate_dense, w_up_dense, w_down_dense))
    wg_moe, wu_moe, wd_moe = (w.astype(BF16) for w in (w_gate_moe, w_up_moe, w_down_moe))

    xs = x.reshape(n, D_MODEL)
    for l in range(depth):
        xs = _mixer(xs, batch, seq, sinks[l], g_mix[l][None], w_in_b[l], gqk[l], gqm[l], bd_qk, bd_mem,
                    bias, conv_w[l], [t[l] for t in mem_kv_all], gout[l], w_out_b[l])
        i = l // 2
        gf = g_ffn[l][None]
        if l % 2 == 0:
            xs = _dense_ffn(xs, gf, i, wg_dense, wu_dense, wd_dense)
        else:
            hn, route_c, route_t, cnt, xs_zeros, ys_zeros = _router(xs, gf, wr_hi[i], wr_lo[i], br[i], ltri)
            a, tile_expert, tile_block, n_used, fast_d, fast_c = _routing_tables(cnt, n)
            x_sorted = _dispatch(hn, route_t, a, fast_d, xs_zeros)
            y_sorted = _moe_ffn(x_sorted, tile_expert, tile_block, n_used, i, wg_moe, wu_moe, wd_moe, ys_zeros)
            xs = _combine(xs, route_c, y_sorted, a, fast_c)
    return xs.reshape(batch, seq, D_MODEL)
```

```python
import functools

import jax
import jax.numpy as jnp
from jax import lax
from jax.experimental import pallas as pl
from jax.experimental.pallas import tpu as pltpu

F32 = jnp.float32
BF16 = jnp.bfloat16

D_MODEL = 1024
CHUNK = 64
HEAD_DIM = 64
SWA_HEADS = 8
SWA_KV_HEADS = 2
SWA_GROUP = SWA_HEADS // SWA_KV_HEADS
CONV_WIDTH = 256
CONV_K = 3
MEM_HEADS = 4
SWA_WIDTH = SWA_HEADS * HEAD_DIM
KV_WIDTH = SWA_KV_HEADS * HEAD_DIM
MEM_WIDTH = MEM_HEADS * HEAD_DIM
QK_WIDTH = SWA_WIDTH + KV_WIDTH
IN_WIDTH = SWA_WIDTH + 2 * KV_WIDTH + 3 * CONV_WIDTH + MEM_WIDTH
D_FF = 3584
N_EXPERTS = 8
EPS = 1e-6

PAIR = 2 * CHUNK
WIN = 4 * CHUNK
HALO = WIN - PAIR
CONV_HALO = 8
LANES = 128
ROUTER_PAD = LANES

VMEM_LIMIT_BYTES = 56 * 1024 * 1024

NT_DIMS = (((1,), (1,)), ((), ()))


def _rms_scale(x, width):
    return lax.rsqrt(jnp.sum(x * x, axis=-1, keepdims=True) * (1.0 / width) + EPS)


def _head_rms(t):
    sq = t * t
    low = lax.broadcasted_iota(jnp.int32, (t.shape[0], LANES), 1) < HEAD_DIM
    scales = []
    for c in range(0, t.shape[1], LANES):
        pair = sq[:, c:c + LANES]
        even = jnp.sum(jnp.where(low, pair, 0.0), axis=-1, keepdims=True)
        odd = jnp.sum(jnp.where(low, 0.0, pair), axis=-1, keepdims=True)
        scales.append(lax.rsqrt(jnp.where(low, even, odd) * (1.0 / HEAD_DIM) + EPS))
    return jnp.concatenate(scales, axis=1)


def _low_half(shape):
    return lax.broadcasted_iota(jnp.int32, shape, len(shape) - 1) % LANES < HEAD_DIM


def _softmax_rows(s, sink=None):
    m = jnp.max(s, axis=-1, keepdims=True)
    if sink is not None:
        m = jnp.maximum(m, sink)
    p = jnp.exp(s - m)
    den = jnp.sum(p, axis=-1, keepdims=True)
    if sink is not None:
        den = den + jnp.exp(sink - m)
    return (p * (1.0 / den)).astype(BF16)


def _memkv_kernel(mem_ref, g_ref, w_ref, gk_ref, mka_ref, mkb_ref, mva_ref, mvb_ref):
    m = mem_ref[...]
    hm = (m * _rms_scale(m, D_MODEL) * g_ref[...]).astype(BF16)
    kv = jnp.dot(hm, w_ref[...], preferred_element_type=F32)
    k = kv[:, :MEM_WIDTH]
    k = k * _head_rms(k) * gk_ref[...]
    v = kv[:, MEM_WIDTH:]
    low = _low_half(k.shape)
    mka_ref[...] = jnp.where(low, k, 0.0).astype(BF16)
    mkb_ref[...] = jnp.where(low, 0.0, k).astype(BF16)
    mva_ref[...] = jnp.where(low, v, 0.0).astype(BF16)
    mvb_ref[...] = jnp.where(low, 0.0, v).astype(BF16)


def _mem_kv(mem2d, g_mem, w_mem_kv, gk_mem):
    depth = g_mem.shape[0]
    rows = mem2d.shape[0]
    tr = min(rows, 512)
    out = jax.ShapeDtypeStruct((depth, rows, MEM_WIDTH), BF16)
    out_spec = pl.BlockSpec((None, tr, MEM_WIDTH), lambda l, i: (l, i, 0))
    return pl.pallas_call(
        _memkv_kernel,
        grid=(depth, rows // tr),
        in_specs=[
            pl.BlockSpec((tr, D_MODEL), lambda l, i: (i, 0)),
            pl.BlockSpec((None, 1, D_MODEL), lambda l, i: (l, 0, 0)),
            pl.BlockSpec((None, D_MODEL, 2 * MEM_WIDTH), lambda l, i: (l, 0, 0)),
            pl.BlockSpec((None, 1, MEM_WIDTH), lambda l, i: (l, 0, 0)),
        ],
        out_specs=[out_spec] * 4,
        out_shape=[out] * 4,
        compiler_params=pltpu.CompilerParams(
            dimension_semantics=("arbitrary", "arbitrary"), vmem_limit_bytes=VMEM_LIMIT_BYTES),
        name="mem_kv",
    )(mem2d, g_mem, w_mem_kv, gk_mem)


MEM_ROWS = 256


def _mixer_kernel(sinks_ref, x_ref, gmix_ref, win_ref, gqk_ref, gqm_ref, bias_ref,
                  convw_ref, mka_ref, mkb_ref, mva_ref, mvb_ref, gout_ref, wout_ref, o_ref,
                  kbuf, vbuf, zbuf, yswa, ymem, sbuf, pbuf, smem, pmem, *, ts):
    s_idx = pl.program_id(1)
    x = x_ref[...]
    h = (x * _rms_scale(x, D_MODEL) * gmix_ref[...]).astype(BF16)
    proj = jnp.dot(h, win_ref[...], preferred_element_type=F32)

    @pl.when(s_idx == 0)
    def _():
        kbuf[:, 0:HALO, :] = jnp.zeros((2 * SWA_KV_HEADS, HALO, LANES), BF16)
        vbuf[:, 0:HALO, :] = jnp.zeros((2 * SWA_KV_HEADS, HALO, LANES), BF16)
        zbuf[0:CONV_HALO, :] = jnp.zeros((CONV_HALO, CONV_WIDTH), F32)

    qk = proj[:, :QK_WIDTH]
    qk = qk * _head_rms(qk) * gqk_ref[...]
    q = qk[:, :SWA_WIDTH].astype(BF16)
    low = _low_half((ts, KV_WIDTH))
    for buf, t in ((kbuf, qk[:, SWA_WIDTH:]), (vbuf, proj[:, QK_WIDTH:QK_WIDTH + KV_WIDTH])):
        swapped = pltpu.roll(t, HEAD_DIM, 1)
        buf[0, HALO:HALO + ts, :] = jnp.where(low, t, 0.0).astype(BF16)
        buf[1, HALO:HALO + ts, :] = jnp.where(low, 0.0, swapped).astype(BF16)
        buf[2, HALO:HALO + ts, :] = jnp.where(low, swapped, 0.0).astype(BF16)
        buf[3, HALO:HALO + ts, :] = jnp.where(low, 0.0, t).astype(BF16)

    key_lane = lax.broadcasted_iota(jnp.int32, (1, WIN), 1)
    first_keys_valid = jnp.logical_or(key_lane >= HALO, s_idx > 0)

    qm = proj[:, IN_WIDTH - MEM_WIDTH:]
    qm = (qm * _head_rms(qm) * gqm_ref[...]).astype(BF16)
    mem_rows = min(ts, MEM_ROWS)
    mem_blocks = [(pr, rb) for pr in range(MEM_HEADS // 2) for rb in range(ts // mem_rows)]

    blocks = [(j, kh, half) for j in range(ts // PAIR) for kh in range(SWA_KV_HEADS) for half in range(2)]
    for n_blk, (j, kh, half) in enumerate(blocks):
        r0 = j * PAIR
        c0 = kh * SWA_GROUP * HEAD_DIM
        qg = jnp.concatenate([q[r0:r0 + PAIR, c0:c0 + LANES],
                              q[r0:r0 + PAIR, c0 + LANES:c0 + 2 * LANES]], axis=0)
        sbuf[n_blk] = lax.dot_general(qg, kbuf[2 * kh + half, r0:r0 + WIN, :], NT_DIMS,
                                      preferred_element_type=F32)
    kbuf[:, 0:HALO, :] = kbuf[:, ts:ts + HALO, :]

    for n_blk, (pr, rb) in enumerate(mem_blocks):
        qp = qm[rb * mem_rows:(rb + 1) * mem_rows, pr * LANES:(pr + 1) * LANES]
        for half, mk_ref in enumerate((mka_ref, mkb_ref)):
            smem[2 * n_blk + half] = lax.dot_general(qp, mk_ref[:, pr * LANES:(pr + 1) * LANES], NT_DIMS,
                                                     preferred_element_type=F32)

    for n_blk, (j, kh, half) in enumerate(blocks):
        for pr in range(2):
            hd = kh * SWA_GROUP + 2 * pr + half
            sg = sbuf[n_blk, pr * PAIR:(pr + 1) * PAIR, :] + bias_ref[hd]
            if j == 0:
                sg = jnp.where(first_keys_valid, sg, -jnp.inf)
            pbuf[n_blk, pr * PAIR:(pr + 1) * PAIR, :] = _softmax_rows(sg, sinks_ref[hd])

    for n_blk in range(2 * len(mem_blocks)):
        pmem[n_blk] = _softmax_rows(smem[n_blk])

    for j in range(ts // PAIR):
        r0 = j * PAIR
        for kh in range(SWA_KV_HEADS):
            c0 = kh * SWA_GROUP * HEAD_DIM
            n_blk = (j * SWA_KV_HEADS + kh) * 2
            o = (jnp.dot(pbuf[n_blk], vbuf[2 * kh, r0:r0 + WIN, :], preferred_element_type=F32)
                 + jnp.dot(pbuf[n_blk + 1], vbuf[2 * kh + 1, r0:r0 + WIN, :], preferred_element_type=F32))
            yswa[r0:r0 + PAIR, c0:c0 + LANES] = o[0:PAIR]
            yswa[r0:r0 + PAIR, c0 + LANES:c0 + 2 * LANES] = o[PAIR:]
    vbuf[:, 0:HALO, :] = vbuf[:, ts:ts + HALO, :]

    for n_blk, (pr, rb) in enumerate(mem_blocks):
        cols = slice(pr * LANES, (pr + 1) * LANES)
        ymem[rb * mem_rows:(rb + 1) * mem_rows, cols] = (
            jnp.dot(pmem[2 * n_blk], mva_ref[:, cols], preferred_element_type=F32)
            + jnp.dot(pmem[2 * n_blk + 1], mvb_ref[:, cols], preferred_element_type=F32))

    c0 = QK_WIDTH + KV_WIDTH
    gate_b = proj[:, c0:c0 + CONV_WIDTH]
    z = proj[:, c0 + CONV_WIDTH:c0 + 2 * CONV_WIDTH] * proj[:, c0 + 2 * CONV_WIDTH:c0 + 3 * CONV_WIDTH]
    zbuf[CONV_HALO:CONV_HALO + ts, :] = z
    z1 = zbuf[CONV_HALO - 1:CONV_HALO - 1 + ts, :]
    z2 = zbuf[CONV_HALO - 2:CONV_HALO - 2 + ts, :]
    cw = convw_ref[...]
    y_conv = gate_b * (cw[0:1] * z2 + cw[1:2] * z1 + cw[2:3] * z)
    zbuf[0:CONV_HALO, :] = zbuf[ts:ts + CONV_HALO, :]

    gout = gout_ref[...]
    ys = yswa[...]
    ym = ymem[...]
    a = (ys * _rms_scale(ys, SWA_WIDTH) * gout[:, :SWA_WIDTH]).astype(BF16)
    b = (y_conv * _rms_scale(y_conv, CONV_WIDTH) * gout[:, SWA_WIDTH:SWA_WIDTH + CONV_WIDTH]).astype(BF16)
    c = (ym * _rms_scale(ym, MEM_WIDTH) * gout[:, SWA_WIDTH + CONV_WIDTH:]).astype(BF16)
    out = x + jnp.dot(a, wout_ref[0:SWA_WIDTH, :], preferred_element_type=F32)
    out = out + jnp.dot(b, wout_ref[SWA_WIDTH:SWA_WIDTH + CONV_WIDTH, :], preferred_element_type=F32)
    out = out + jnp.dot(c, wout_ref[SWA_WIDTH + CONV_WIDTH:, :], preferred_element_type=F32)
    o_ref[...] = out


def _mixer(x2d, batch, seq, sinks, gmix, w_in, gqk, gqm, bias, conv_w, mem_kv, gout, w_out):
    ts = min(seq, 1024)
    nseq = seq // ts
    mem_len = mem_kv[0].shape[0] // batch
    mem_rows = min(ts, MEM_ROWS)
    const = lambda b, s, sk: (0, 0)
    mem_spec = pl.BlockSpec((mem_len, MEM_WIDTH), lambda b, s, sk: (b, 0))
    grid_spec = pltpu.PrefetchScalarGridSpec(
        num_scalar_prefetch=1,
        grid=(batch, nseq),
        in_specs=[
            pl.BlockSpec((ts, D_MODEL), lambda b, s, sk: (b * nseq + s, 0)),
            pl.BlockSpec((1, D_MODEL), const),
            pl.BlockSpec((D_MODEL, IN_WIDTH), const),
            pl.BlockSpec((1, QK_WIDTH), const),
            pl.BlockSpec((1, MEM_WIDTH), const),
            pl.BlockSpec((SWA_HEADS, PAIR, WIN), lambda b, s, sk: (0, 0, 0)),
            pl.BlockSpec((CONV_K, CONV_WIDTH), const),
            mem_spec, mem_spec, mem_spec, mem_spec,
            pl.BlockSpec((1, D_MODEL), const),
            pl.BlockSpec((D_MODEL, D_MODEL), const),
        ],
        out_specs=pl.BlockSpec((ts, D_MODEL), lambda b, s, sk: (b * nseq + s, 0)),
        scratch_shapes=[
            pltpu.VMEM((2 * SWA_KV_HEADS, ts + HALO, LANES), BF16),
            pltpu.VMEM((2 * SWA_KV_HEADS, ts + HALO, LANES), BF16),
            pltpu.VMEM((ts + CONV_HALO, CONV_WIDTH), F32),
            pltpu.VMEM((ts, SWA_WIDTH), F32),
            pltpu.VMEM((ts, MEM_WIDTH), F32),
            pltpu.VMEM((ts // PAIR * 2 * SWA_KV_HEADS, 2 * PAIR, WIN), F32),
            pltpu.VMEM((ts // PAIR * 2 * SWA_KV_HEADS, 2 * PAIR, WIN), BF16),
            pltpu.VMEM((MEM_HEADS * (ts // mem_rows), mem_rows, mem_len), F32),
            pltpu.VMEM((MEM_HEADS * (ts // mem_rows), mem_rows, mem_len), BF16),
        ],
    )
    return pl.pallas_call(
        functools.partial(_mixer_kernel, ts=ts),
        grid_spec=grid_spec,
        out_shape=jax.ShapeDtypeStruct(x2d.shape, F32),
        compiler_params=pltpu.CompilerParams(
            dimension_semantics=("arbitrary", "arbitrary"), vmem_limit_bytes=VMEM_LIMIT_BYTES),
        name="token_mixer",
    )(sinks, x2d, gmix, w_in, gqk, gqm, bias, conv_w, *mem_kv, gout, w_out)


FF_CHUNK = 256


def _swiglu_accumulate(h, wg_ref, wu_ref, wd_ref, acc_ref):
    tf = wg_ref.shape[-1]
    for c in range(tf // FF_CHUNK):
        sl = slice(c * FF_CHUNK, (c + 1) * FF_CHUNK)
        gate = jnp.dot(h, wg_ref[:, sl], preferred_element_type=F32)
        up = jnp.dot(h, wu_ref[:, sl], preferred_element_type=F32)
        act = (gate * jax.nn.sigmoid(gate) * up).astype(BF16)
        acc_ref[...] += jnp.dot(act, wd_ref[sl, :], preferred_element_type=F32)


def _dense_ffn_kernel(x_ref, g_ref, wg_ref, wu_ref, wd_ref, cg_ref, cu_ref, cd_ref,
                      o_ref, og_ref, ou_ref, od_ref):
    og_ref[...] = cg_ref[...].astype(BF16)
    ou_ref[...] = cu_ref[...].astype(BF16)
    od_ref[...] = cd_ref[...].astype(BF16)
    x = x_ref[...]
    h = (x * _rms_scale(x, D_MODEL) * g_ref[...]).astype(BF16)
    o_ref[...] = x
    _swiglu_accumulate(h, wg_ref, wu_ref, wd_ref, o_ref)


def _dense_ffn(x2d, g, layer, wg, wu, wd, ex_gate, ex_up, ex_down):
    n = x2d.shape[0]
    tm = min(n, 512)
    steps = n // tm
    rows_gu = ex_gate.shape[1] // steps
    rows_d = ex_down.shape[1] // steps
    resident = functools.partial(pl.BlockSpec, pipeline_mode=pl.Buffered(1))
    cast_gu_in = pl.BlockSpec((None, rows_gu, D_FF), lambda i: (layer, i, 0))
    cast_gu_out = pl.BlockSpec((rows_gu, D_FF), lambda i: (i, 0))
    return pl.pallas_call(
        _dense_ffn_kernel,
        grid=(steps,),
        in_specs=[
            pl.BlockSpec((tm, D_MODEL), lambda i: (i, 0)),
            pl.BlockSpec((1, D_MODEL), lambda i: (0, 0)),
            resident((None, D_MODEL, D_FF), lambda i: (layer, 0, 0)),
            resident((None, D_MODEL, D_FF), lambda i: (layer, 0, 0)),
            resident((None, D_FF, D_MODEL), lambda i: (layer, 0, 0)),
            cast_gu_in,
            cast_gu_in,
            pl.BlockSpec((None, rows_d, D_MODEL), lambda i: (layer, i, 0)),
        ],
        out_specs=[
            pl.BlockSpec((tm, D_MODEL), lambda i: (i, 0)),
            cast_gu_out,
            cast_gu_out,
            pl.BlockSpec((rows_d, D_MODEL), lambda i: (i, 0)),
        ],
        out_shape=[jax.ShapeDtypeStruct(x2d.shape, F32),
                   jax.ShapeDtypeStruct(ex_gate.shape[1:], BF16),
                   jax.ShapeDtypeStruct(ex_up.shape[1:], BF16),
                   jax.ShapeDtypeStruct(ex_down.shape[1:], BF16)],
        compiler_params=pltpu.CompilerParams(
            dimension_semantics=("arbitrary",), vmem_limit_bytes=VMEM_LIMIT_BYTES),
        name="dense_ffn",
    )(x2d, g, wg, wu, wd, ex_gate, ex_up, ex_down)


MOE_TT = 256
MOE_TM = 512
ROW_ALIGN = 16
MOE_WIN = MOE_TT + ROW_ALIGN
DISPATCH_FAST_WIN = 144
DISPATCH_FAST_FILL = DISPATCH_FAST_WIN - ROW_ALIGN
COMBINE_FAST_WIN = LANES
REGION_SLACK = MOE_WIN
ROUTE_ROWS = 8
ROUTER_ROWS = 1024


def _moe_ffn_kernel(te_ref, tb_ref, nu_ref, x_ref, wg_ref, wu_ref, wd_ref, ys_hbm, o_ref, acc_ref):
    del te_ref, tb_ref, ys_hbm
    f = pl.program_id(1)
    used = pl.program_id(0) < nu_ref[0]

    @pl.when(jnp.logical_and(used, f == 0))
    def _():
        acc_ref[...] = jnp.zeros(acc_ref.shape, F32)

    @pl.when(used)
    def _():
        _swiglu_accumulate(x_ref[...], wg_ref, wu_ref, wd_ref, acc_ref)

    @pl.when(jnp.logical_and(used, f == pl.num_programs(1) - 1))
    def _():
        o_ref[...] = acc_ref[...].astype(BF16)


def _moe_ffn(xs, tile_expert, tile_block, n_used, wg, wu, wd, ys):
    p = xs.shape[0]
    tm = MOE_TM
    tf = D_FF
    nf = D_FF // tf
    max_tiles = tile_expert.shape[0]

    def f_eff(i, f, nu):
        return jnp.where(i < nu[0], f, nf - 1)

    grid_spec = pltpu.PrefetchScalarGridSpec(
        num_scalar_prefetch=3,
        grid=(max_tiles, nf),
        in_specs=[
            pl.BlockSpec((tm, D_MODEL), lambda i, f, te, tb, nu: (tb[i], 0)),
            pl.BlockSpec((None, D_MODEL, tf), lambda i, f, te, tb, nu: (te[i], 0, f_eff(i, f, nu))),
            pl.BlockSpec((None, D_MODEL, tf), lambda i, f, te, tb, nu: (te[i], 0, f_eff(i, f, nu))),
            pl.BlockSpec((None, tf, D_MODEL), lambda i, f, te, tb, nu: (te[i], f_eff(i, f, nu), 0)),
            pl.BlockSpec(memory_space=pl.ANY),
        ],
        out_specs=pl.BlockSpec((tm, D_MODEL), lambda i, f, te, tb, nu: (tb[i], 0)),
        scratch_shapes=[pltpu.VMEM((tm, D_MODEL), F32)],
    )
    return pl.pallas_call(
        _moe_ffn_kernel,
        grid_spec=grid_spec,
        out_shape=jax.ShapeDtypeStruct((p, D_MODEL), BF16),
        input_output_aliases={7: 0},
        compiler_params=pltpu.CompilerParams(
            dimension_semantics=("arbitrary", "arbitrary"), vmem_limit_bytes=VMEM_LIMIT_BYTES),
        name="moe_ffn",
    )(tile_expert, tile_block, n_used, xs, wg, wu, wd, ys)


def _router_kernel(x_ref, g_ref, wh_ref, wl_ref, b_ref, ltri_ref, hn_ref, rc_ref, rt_ref, cnt_ref, xs0_ref, ys0_ref):
    xs0_ref[...] = jnp.zeros(xs0_ref.shape, BF16)
    ys0_ref[...] = jnp.zeros(ys0_ref.shape, BF16)
    x = x_ref[...]
    h = x * _rms_scale(x, D_MODEL) * g_ref[...]
    hh = h.astype(BF16)
    hn_ref[...] = hh
    hl = (h - hh.astype(F32)).astype(BF16)
    wh = wh_ref[...]
    logits = (jnp.dot(hh, wh, preferred_element_type=F32)
              + jnp.dot(hl, wh, preferred_element_type=F32)
              + jnp.dot(hh, wl_ref[...], preferred_element_type=F32)) + b_ref[...]
    lane = lax.broadcasted_iota(jnp.int32, (MOE_TT, ROUTER_PAD), 1)
    for blk in range(x.shape[0] // MOE_TT):
        rows = slice(blk * MOE_TT, (blk + 1) * MOE_TT)
        lg = logits[rows]
        v1 = jnp.max(lg, axis=-1, keepdims=True)
        i1 = jnp.min(jnp.where(lg == v1, lane, ROUTER_PAD), axis=-1, keepdims=True)
        rest = jnp.where(lane == i1, -jnp.inf, lg)
        v2 = jnp.max(rest, axis=-1, keepdims=True)
        i2 = jnp.min(jnp.where(rest == v2, lane, ROUTER_PAD), axis=-1, keepdims=True)
        e2 = jnp.exp(v2 - v1)
        den = 1.0 + e2
        chosen = jnp.logical_or(lane == i1, lane == i2)
        cum = jnp.dot(ltri_ref[...], jnp.where(chosen, 1.0, 0.0).astype(BF16), preferred_element_type=F32)
        r1 = jnp.sum(jnp.where(lane == i1, cum, 0.0), axis=-1, keepdims=True) - 1.0
        r2 = jnp.sum(jnp.where(lane == i2, cum, 0.0), axis=-1, keepdims=True) - 1.0
        rc = jnp.zeros(lg.shape, F32)
        for k, col in enumerate((i1.astype(F32), i2.astype(F32), r1, r2, 1.0 / den, e2 / den)):
            rc = jnp.where(lane == k, col, rc)
        rc_ref[rows, :] = rc
        rt_ref[:, rows] = rc.T[:ROUTE_ROWS, :]
        cnt_ref[blk] = cum[MOE_TT - 1:MOE_TT, :].astype(jnp.int32)


def _router(x2d, g, w_hi, w_lo, b_pad, ltri):
    n = x2d.shape[0]
    tr = min(n, ROUTER_ROWS)
    tiles = tr // MOE_TT
    sorted_rows = _sorted_rows(n)
    zero_rows = sorted_rows // (n // tr)
    return pl.pallas_call(
        _router_kernel,
        grid=(n // tr,),
        in_specs=[
            pl.BlockSpec((tr, D_MODEL), lambda i: (i, 0)),
            pl.BlockSpec((1, D_MODEL), lambda i: (0, 0)),
            pl.BlockSpec((D_MODEL, ROUTER_PAD), lambda i: (0, 0)),
            pl.BlockSpec((D_MODEL, ROUTER_PAD), lambda i: (0, 0)),
            pl.BlockSpec((1, ROUTER_PAD), lambda i: (0, 0)),
            pl.BlockSpec((MOE_TT, MOE_TT), lambda i: (0, 0)),
        ],
        out_specs=[
            pl.BlockSpec((tr, D_MODEL), lambda i: (i, 0)),
            pl.BlockSpec((tr, ROUTER_PAD), lambda i: (i, 0)),
            pl.BlockSpec((ROUTE_ROWS, tr), lambda i: (0, i)),
            pl.BlockSpec((tiles, 1, ROUTER_PAD), lambda i: (i, 0, 0)),
            pl.BlockSpec((zero_rows, D_MODEL), lambda i: (i, 0)),
            pl.BlockSpec((zero_rows, D_MODEL), lambda i: (i, 0)),
        ],
        out_shape=[jax.ShapeDtypeStruct((n, D_MODEL), BF16),
                   jax.ShapeDtypeStruct((n, ROUTER_PAD), F32),
                   jax.ShapeDtypeStruct((ROUTE_ROWS, n), F32),
                   jax.ShapeDtypeStruct((n // MOE_TT, 1, ROUTER_PAD), jnp.int32),
                   jax.ShapeDtypeStruct((sorted_rows, D_MODEL), BF16),
                   jax.ShapeDtypeStruct((sorted_rows, D_MODEL), BF16)],
        compiler_params=pltpu.CompilerParams(
            dimension_semantics=("arbitrary",), vmem_limit_bytes=VMEM_LIMIT_BYTES),
        name="router",
    )(x2d, g, w_hi, w_lo, b_pad, ltri)


def _routing_tables(cnt, n):
    i32 = jnp.int32
    nt = n // MOE_TT
    cnt = cnt.reshape(nt, ROUTER_PAD)[:, :N_EXPERTS]
    total = jnp.sum(cnt, axis=0)
    region = ((total + REGION_SLACK + MOE_TM - 1) // MOE_TM) * MOE_TM
    start = jnp.cumsum(region) - region
    first = start[None, :] + jnp.cumsum(cnt, axis=0) - cnt
    a = jnp.concatenate([first, (start + total)[None, :]], axis=0).reshape(-1).astype(i32)
    ntile = (total + MOE_TM - 1) // MOE_TM
    tend = jnp.cumsum(ntile)
    n_used = tend[-1]
    max_tiles = 2 * n // MOE_TM + N_EXPERTS
    tile = jnp.minimum(jnp.arange(max_tiles, dtype=i32), n_used - 1)
    tile_expert = jnp.minimum(jnp.sum((tend[None, :] <= tile[:, None]).astype(i32), axis=1), N_EXPERTS - 1)
    tile_block = start[tile_expert] // MOE_TM + tile - (tend - ntile)[tile_expert]
    fill = jnp.max(jnp.bitwise_and(first, ROW_ALIGN - 1) + cnt, axis=1)
    fast_dispatch = jnp.logical_and(fill <= DISPATCH_FAST_FILL, jnp.arange(nt) < nt - 1).astype(i32)
    fast_combine = (fill <= COMBINE_FAST_WIN).astype(i32)
    return (a, tile_expert.astype(i32), tile_block.astype(i32), n_used.reshape(1).astype(i32),
            fast_dispatch, fast_combine)


def _sorted_rows(n):
    return -(-(2 * n + N_EXPERTS * (REGION_SLACK + MOE_TM)) // MOE_TM) * MOE_TM


def _window(a_ref, t, e):
    a = a_ref[t * N_EXPERTS + e]
    off = jnp.bitwise_and(a, ROW_ALIGN - 1)
    return pl.multiple_of(a - off, ROW_ALIGN), off


def _expert_slot(e, e1, e2, r1, r2, off):
    d = jnp.where(e1 == e, r1, jnp.where(e2 == e, r2, -1.0))
    return jnp.where(d >= 0.0, d + off.astype(F32), -1.0).astype(jnp.int32)


def _dispatch_kernel(a_ref, fast_ref, hn_ref, rt_ref, zeros_hbm, xs_hbm, carry, stage, sem):
    del zeros_hbm
    t = pl.program_id(0)
    nt = pl.num_programs(0)
    slot = t % 2

    def window_copy(sl, e, base, win):
        return pltpu.make_async_copy(stage.at[sl, e, 0:win], xs_hbm.at[pl.ds(base, win)], sem)

    @pl.when(t == 0)
    def _():
        carry[...] = jnp.zeros(carry.shape, F32)

    def stage_and_send(win, prev_win_is_fast):
        hn = hn_ref[...]
        rt = rt_ref[...]
        e1, e2, r1, r2 = rt[0:1], rt[1:2], rt[2:3], rt[3:4]
        row = lax.broadcasted_iota(jnp.int32, (win, MOE_TT), 0)
        head_row = lax.broadcasted_iota(jnp.int32, (ROW_ALIGN, 1), 0)
        bases = []
        for e in range(N_EXPERTS):
            base, off = _window(a_ref, t, e)
            bases.append(base)
            slot_of = _expert_slot(e, e1, e2, r1, r2, off)
            sel = jnp.where(row == slot_of, 1.0, 0.0).astype(BF16)
            rows = jnp.dot(sel, hn, preferred_element_type=F32)
            head = jnp.where(head_row < off, carry[e], rows[0:ROW_ALIGN])
            stage[slot, e, 0:ROW_ALIGN, :] = head.astype(BF16)
            stage[slot, e, ROW_ALIGN:win, :] = rows[ROW_ALIGN:].astype(BF16)
            filled = off + a_ref[(t + 1) * N_EXPERTS + e] - a_ref[t * N_EXPERTS + e]
            last_group = pl.multiple_of(lax.shift_right_logical(filled, 4) * ROW_ALIGN, ROW_ALIGN)
            carry[e] = stage[slot, e, pl.ds(last_group, ROW_ALIGN), :].astype(F32)

        for was_fast, prev_win in ((1, DISPATCH_FAST_WIN), (0, MOE_WIN)):
            @pl.when(jnp.logical_and(t > 0, prev_win_is_fast == was_fast))
            def _():
                for e in range(N_EXPERTS):
                    window_copy(1 - slot, e, 0, prev_win).wait()

        for e in range(N_EXPERTS):
            window_copy(slot, e, bases[e], win).start()

    is_fast = fast_ref[t]
    prev_fast = fast_ref[jnp.maximum(t - 1, 0)]

    @pl.when(is_fast == 1)
    def _():
        stage_and_send(DISPATCH_FAST_WIN, prev_fast)

    @pl.when(is_fast == 0)
    def _():
        stage_and_send(MOE_WIN, prev_fast)

    @pl.when(t == nt - 1)
    def _():
        for e in range(N_EXPERTS):
            window_copy(slot, e, 0, MOE_WIN).wait()


def _dispatch(hn, route_t, a, fast, zeros):
    n = hn.shape[0]
    tt = MOE_TT
    grid_spec = pltpu.PrefetchScalarGridSpec(
        num_scalar_prefetch=2,
        grid=(n // tt,),
        in_specs=[
            pl.BlockSpec((tt, D_MODEL), lambda t, a, fs: (t, 0)),
            pl.BlockSpec((ROUTE_ROWS, tt), lambda t, a, fs: (0, t)),
            pl.BlockSpec(memory_space=pl.ANY),
        ],
        out_specs=pl.BlockSpec(memory_space=pl.ANY),
        scratch_shapes=[
            pltpu.VMEM((N_EXPERTS, ROW_ALIGN, D_MODEL), F32),
            pltpu.VMEM((2, N_EXPERTS, MOE_WIN, D_MODEL), BF16),
            pltpu.SemaphoreType.DMA(()),
        ],
    )
    return pl.pallas_call(
        _dispatch_kernel,
        grid_spec=grid_spec,
        out_shape=jax.ShapeDtypeStruct(zeros.shape, BF16),
        input_output_aliases={4: 0},
        compiler_params=pltpu.CompilerParams(
            dimension_semantics=("arbitrary",), vmem_limit_bytes=VMEM_LIMIT_BYTES),
        name="moe_dispatch",
    )(a, fast, hn, route_t, zeros)


def _combine_kernel(a_ref, fast_ref, x_ref, rc_ref, ys_hbm, o_ref, ybuf, yfast, sem):
    t = pl.program_id(0)
    nt = pl.num_programs(0)
    slot = t % 2
    x = x_ref[...]
    rc = rc_ref[...]
    e1, e2, r1, r2, g1, g2 = (rc[:, k:k + 1] for k in range(6))

    def wide_copy(tile, sl, e):
        base, _ = _window(a_ref, tile, e)
        return pltpu.make_async_copy(ys_hbm.at[pl.ds(base, MOE_WIN)], ybuf.at[sl, e], sem.at[sl])

    def narrow_copy(tile, sl, e):
        base, _ = _window(a_ref, tile, e)
        return pltpu.make_async_copy(ys_hbm.at[pl.ds(base, COMBINE_FAST_WIN)],
                                     yfast.at[sl, pl.ds(e * COMBINE_FAST_WIN, COMBINE_FAST_WIN)], sem.at[sl])

    def start_windows(tile, sl):
        for flag, copy in ((1, narrow_copy), (0, wide_copy)):
            @pl.when(fast_ref[tile] == flag)
            def _():
                for e in range(N_EXPERTS):
                    copy(tile, sl, e).start()

    @pl.when(t == 0)
    def _():
        start_windows(0, 0)

    @pl.when(t + 1 < nt)
    def _():
        start_windows(t + 1, 1 - slot)

    @pl.when(fast_ref[t] == 0)
    def _():
        for e in range(N_EXPERTS):
            wide_copy(t, slot, e).wait()
        lane = lax.broadcasted_iota(jnp.int32, (MOE_TT, MOE_WIN), 1)
        acc = jnp.zeros((MOE_TT, D_MODEL), F32)
        for e in range(N_EXPERTS):
            _, off = _window(a_ref, t, e)
            slot_of = _expert_slot(e, e1, e2, r1, r2, off)
            sel = jnp.where(lane == slot_of, 1.0, 0.0).astype(BF16)
            gate = jnp.where(e1 == e, g1, jnp.where(e2 == e, g2, 0.0))
            acc = acc + gate * jnp.dot(sel, ybuf[slot, e], preferred_element_type=F32)
        o_ref[...] = x + acc

    @pl.when(fast_ref[t] == 1)
    def _():
        for e in range(N_EXPERTS):
            narrow_copy(t, slot, e).wait()
        shape = (MOE_TT, COMBINE_FAST_WIN)
        lane = lax.broadcasted_iota(jnp.int32, shape, 1).astype(F32)
        offs = [_window(a_ref, t, e)[1].astype(F32) for e in range(N_EXPERTS)]
        picked = []
        for ek, rk in ((e1, r1), (e2, r2)):
            eb = jnp.broadcast_to(ek, shape)
            pos = jnp.broadcast_to(rk, shape)
            for e in range(N_EXPERTS):
                pos = pos + jnp.where(eb == e, offs[e], 0.0)
            hit = lane == pos
            sel = jnp.concatenate(
                [jnp.where(jnp.logical_and(eb == e, hit), 1.0, 0.0).astype(BF16) for e in range(N_EXPERTS)],
                axis=1)
            picked.append(jnp.dot(sel, yfast[slot], preferred_element_type=F32))
        o_ref[...] = x + (g1 * picked[0] + g2 * picked[1])


def _combine(x2d, route_c, ys, a, fast):
    n = x2d.shape[0]
    tt = MOE_TT
    grid_spec = pltpu.PrefetchScalarGridSpec(
        num_scalar_prefetch=2,
        grid=(n // tt,),
        in_specs=[
            pl.BlockSpec((tt, D_MODEL), lambda t, a, fs: (t, 0)),
            pl.BlockSpec((tt, ROUTER_PAD), lambda t, a, fs: (t, 0)),
            pl.BlockSpec(memory_space=pl.ANY),
        ],
        out_specs=pl.BlockSpec((tt, D_MODEL), lambda t, a, fs: (t, 0)),
        scratch_shapes=[
            pltpu.VMEM((2, N_EXPERTS, MOE_WIN, D_MODEL), BF16),
            pltpu.VMEM((2, N_EXPERTS * COMBINE_FAST_WIN, D_MODEL), BF16),
            pltpu.SemaphoreType.DMA((2,)),
        ],
    )
    return pl.pallas_call(
        _combine_kernel,
        grid_spec=grid_spec,
        out_shape=jax.ShapeDtypeStruct((n, D_MODEL), F32),
        compiler_params=pltpu.CompilerParams(
            dimension_semantics=("arbitrary",), vmem_limit_bytes=VMEM_LIMIT_BYTES),
        name="moe_combine",
    )(a, fast, x2d, route_c, ys)


def _swa_bias():
    qi = jnp.arange(PAIR)[:, None]
    kj = jnp.arange(WIN)[None, :]
    dist = jnp.abs(qi + HALO - kj).astype(F32)
    kc = kj // CHUNK
    qc = qi // CHUNK
    visible = jnp.logical_and(kc >= qc, kc <= qc + 2)
    slopes = jnp.asarray([2.0 ** (-8.0 * (i + 1) / SWA_HEADS) for i in range(SWA_HEADS)], F32)
    bias = -slopes[:, None, None] * dist[None]
    return jnp.where(visible[None], bias, -jnp.inf)


def kernel(x, mem, g_mix, w_in, g_q_swa, g_k_swa, sinks, conv_w, g_mem, w_mem_kv, g_q_mem, g_k_mem,
           g_out_swa, g_out_conv, g_out_mem, w_out, g_ffn, w_gate_dense, w_up_dense, w_down_dense,
           w_router, b_router, w_gate_moe, w_up_moe, w_down_moe):
    batch, seq, _ = x.shape
    depth = g_mix.shape[0]
    n = batch * seq
    scale = HEAD_DIM ** -0.5

    bias = _swa_bias()
    gqk = jnp.concatenate([jnp.tile(g_q_swa * scale, (1, SWA_HEADS)),
                           jnp.tile(g_k_swa, (1, SWA_KV_HEADS))], axis=1)[:, None, :]
    gqm = jnp.tile(g_q_mem * scale, (1, MEM_HEADS))[:, None, :]
    gkm = jnp.tile(g_k_mem, (1, MEM_HEADS))[:, None, :]
    gout = jnp.concatenate([g_out_swa, g_out_conv, g_out_mem], axis=1)[:, None, :]

    w_in_b = w_in.astype(BF16)
    w_out_b = w_out.astype(BF16)
    mem_kv_all = _mem_kv(mem.reshape(-1, D_MODEL), g_mem[:, None, :], w_mem_kv.astype(BF16), gkm)

    wr = jnp.pad(w_router, ((0, 0), (0, 0), (0, ROUTER_PAD - N_EXPERTS)))
    wr_hi = wr.astype(BF16)
    wr_lo = (wr - wr_hi.astype(F32)).astype(BF16)
    br = jnp.pad(b_router, ((0, 0), (0, ROUTER_PAD - N_EXPERTS)), constant_values=-jnp.inf)[:, None, :]
    ltri = (jnp.arange(MOE_TT)[:, None] >= jnp.arange(MOE_TT)[None, :]).astype(BF16)

    wg_dense, wu_dense, wd_dense = (w.astype(BF16) for w in (w_gate_dense, w_up_dense, w_down_dense))
    n_moe, n_exp = w_gate_moe.shape[:2]
    assert depth % 2 == 0 and n_moe == depth // 2, "every expert layer must follow a dense layer"
    ex_gate = w_gate_moe.reshape(n_moe, n_exp * D_MODEL, D_FF)
    ex_up = w_up_moe.reshape(n_moe, n_exp * D_MODEL, D_FF)
    ex_down = w_down_moe.reshape(n_moe, n_exp * D_FF, D_MODEL)

    xs = x.reshape(n, D_MODEL)
    for l in range(depth):
        xs = _mixer(xs, batch, seq, sinks[l], g_mix[l][None], w_in_b[l], gqk[l], gqm[l],
                    bias, conv_w[l], [t[l] for t in mem_kv_all], gout[l], w_out_b[l])
        i = l // 2
        gf = g_ffn[l][None]
        if l % 2 == 0:
            xs, wg_moe, wu_moe, wd_moe = _dense_ffn(xs, gf, i, wg_dense, wu_dense, wd_dense, ex_gate, ex_up, ex_down)
            wg_moe = wg_moe.reshape(n_exp, D_MODEL, D_FF)
            wu_moe = wu_moe.reshape(n_exp, D_MODEL, D_FF)
            wd_moe = wd_moe.reshape(n_exp, D_FF, D_MODEL)
        else:
            hn, route_c, route_t, cnt, xs_zeros, ys_zeros = _router(xs, gf, wr_hi[i], wr_lo[i], br[i], ltri)
            a, tile_expert, tile_block, n_used, fast_d, fast_c = _routing_tables(cnt, n)
            x_sorted = _dispatch(hn, route_t, a, fast_d, xs_zeros)
            y_sorted = _moe_ffn(x_sorted, tile_expert, tile_block, n_used, wg_moe, wu_moe, wd_moe, ys_zeros)
            xs = _combine(xs, route_c, y_sorted, a, fast_c)
    return xs.reshape(batch, seq, D_MODEL)
```

```python
import functools

import jax
import jax.numpy as jnp
from jax import lax
from jax.experimental import pallas as pl
from jax.experimental.pallas import tpu as pltpu

F32 = jnp.float32
BF16 = jnp.bfloat16

D_MODEL = 1024
CHUNK = 64
HEAD_DIM = 64
SWA_HEADS = 8
SWA_KV_HEADS = 2
SWA_GROUP = SWA_HEADS // SWA_KV_HEADS
CONV_WIDTH = 256
CONV_K = 3
MEM_HEADS = 4
SWA_WIDTH = SWA_HEADS * HEAD_DIM
KV_WIDTH = SWA_KV_HEADS * HEAD_DIM
MEM_WIDTH = MEM_HEADS * HEAD_DIM
QK_WIDTH = SWA_WIDTH + KV_WIDTH
IN_WIDTH = SWA_WIDTH + 2 * KV_WIDTH + 3 * CONV_WIDTH + MEM_WIDTH
D_FF = 3584
N_EXPERTS = 8
EPS = 1e-6

PAIR = 2 * CHUNK
WIN = 4 * CHUNK
HALO = WIN - PAIR
CONV_HALO = 8
LANES = 128
ROUTER_PAD = LANES

VMEM_LIMIT_BYTES = 56 * 1024 * 1024

NT_DIMS = (((1,), (1,)), ((), ()))


def _rms_scale(x, width):
    return lax.rsqrt(jnp.sum(x * x, axis=-1, keepdims=True) * (1.0 / width) + EPS)


def _head_rms(t):
    sq = t * t
    low = lax.broadcasted_iota(jnp.int32, (t.shape[0], LANES), 1) < HEAD_DIM
    scales = []
    for c in range(0, t.shape[1], LANES):
        pair = sq[:, c:c + LANES]
        even = jnp.sum(jnp.where(low, pair, 0.0), axis=-1, keepdims=True)
        odd = jnp.sum(jnp.where(low, 0.0, pair), axis=-1, keepdims=True)
        scales.append(lax.rsqrt(jnp.where(low, even, odd) * (1.0 / HEAD_DIM) + EPS))
    return jnp.concatenate(scales, axis=1)


def _low_half(shape):
    return lax.broadcasted_iota(jnp.int32, shape, len(shape) - 1) % LANES < HEAD_DIM


def _softmax_rows(s, sink=None):
    m = jnp.max(s, axis=-1, keepdims=True)
    if sink is not None:
        m = jnp.maximum(m, sink)
    p = jnp.exp(s - m)
    den = jnp.sum(p, axis=-1, keepdims=True)
    if sink is not None:
        den = den + jnp.exp(sink - m)
    return (p * (1.0 / den)).astype(BF16)


def _memkv_kernel(mem_ref, g_ref, w_ref, gk_ref, mka_ref, mkb_ref, mva_ref, mvb_ref):
    m = mem_ref[...]
    hm = (m * _rms_scale(m, D_MODEL) * g_ref[...]).astype(BF16)
    kv = jnp.dot(hm, w_ref[...], preferred_element_type=F32)
    k = kv[:, :MEM_WIDTH]
    k = k * _head_rms(k) * gk_ref[...]
    v = kv[:, MEM_WIDTH:]
    low = _low_half(k.shape)
    mka_ref[...] = jnp.where(low, k, 0.0).astype(BF16)
    mkb_ref[...] = jnp.where(low, 0.0, k).astype(BF16)
    mva_ref[...] = jnp.where(low, v, 0.0).astype(BF16)
    mvb_ref[...] = jnp.where(low, 0.0, v).astype(BF16)


def _mem_kv(mem2d, g_mem, w_mem_kv, gk_mem):
    depth = g_mem.shape[0]
    rows = mem2d.shape[0]
    tr = min(rows, 512)
    out = jax.ShapeDtypeStruct((depth, rows, MEM_WIDTH), BF16)
    out_spec = pl.BlockSpec((None, tr, MEM_WIDTH), lambda l, i: (l, i, 0))
    return pl.pallas_call(
        _memkv_kernel,
        grid=(depth, rows // tr),
        in_specs=[
            pl.BlockSpec((tr, D_MODEL), lambda l, i: (i, 0)),
            pl.BlockSpec((None, 1, D_MODEL), lambda l, i: (l, 0, 0)),
            pl.BlockSpec((None, D_MODEL, 2 * MEM_WIDTH), lambda l, i: (l, 0, 0)),
            pl.BlockSpec((None, 1, MEM_WIDTH), lambda l, i: (l, 0, 0)),
        ],
        out_specs=[out_spec] * 4,
        out_shape=[out] * 4,
        compiler_params=pltpu.CompilerParams(
            dimension_semantics=("arbitrary", "arbitrary"), vmem_limit_bytes=VMEM_LIMIT_BYTES),
        name="mem_kv",
    )(mem2d, g_mem, w_mem_kv, gk_mem)


MEM_ROWS = 256
MIX_SEQS = 2
MIX_LAG = 2


def _mixer_kernel(sinks_ref, x_ref, gmix_ref, win_ref, gqk_ref, gqm_ref, bias_ref,
                  convw_ref, mka_ref, mkb_ref, mva_ref, mvb_ref, gout_ref, wout_ref, o_ref,
                  kbuf, vbuf, zbuf, yswa, ymem, sbuf, pbuf, smem, pmem, *, ts, nseqs):
    s_idx = pl.program_id(1)
    mem_rows = min(ts, MEM_ROWS)
    mem_blocks = [(pr, rb) for pr in range(MEM_HEADS // 2) for rb in range(ts // mem_rows)]
    blocks = [(j, kh, half) for j in range(ts // PAIR) for kh in range(SWA_KV_HEADS) for half in range(2)]
    key_lane = lax.broadcasted_iota(jnp.int32, (1, WIN), 1)
    first_keys_valid = jnp.logical_or(key_lane >= HALO, s_idx > 0)
    live = [dict() for _ in range(nseqs)]

    @pl.when(s_idx == 0)
    def _():
        kbuf[:, :, 0:HALO, :] = jnp.zeros((nseqs, 2 * SWA_KV_HEADS, HALO, LANES), BF16)
        vbuf[:, :, 0:HALO, :] = jnp.zeros((nseqs, 2 * SWA_KV_HEADS, HALO, LANES), BF16)
        zbuf[:, 0:CONV_HALO, :] = jnp.zeros((nseqs, CONV_HALO, CONV_WIDTH), F32)

    def project(k):
        x = x_ref[k]
        h = (x * _rms_scale(x, D_MODEL) * gmix_ref[...]).astype(BF16)
        live[k]["proj"] = jnp.dot(h, win_ref[...], preferred_element_type=F32)

    def prepare(k):
        proj = live[k]["proj"]
        qk = proj[:, :QK_WIDTH]
        qk = qk * _head_rms(qk) * gqk_ref[...]
        live[k]["q"] = qk[:, :SWA_WIDTH].astype(BF16)
        low = _low_half((ts, KV_WIDTH))
        for buf, t in ((kbuf, qk[:, SWA_WIDTH:]), (vbuf, proj[:, QK_WIDTH:QK_WIDTH + KV_WIDTH])):
            swapped = pltpu.roll(t, HEAD_DIM, 1)
            buf[k, 0, HALO:HALO + ts, :] = jnp.where(low, t, 0.0).astype(BF16)
            buf[k, 1, HALO:HALO + ts, :] = jnp.where(low, 0.0, swapped).astype(BF16)
            buf[k, 2, HALO:HALO + ts, :] = jnp.where(low, swapped, 0.0).astype(BF16)
            buf[k, 3, HALO:HALO + ts, :] = jnp.where(low, 0.0, t).astype(BF16)
        qm = proj[:, IN_WIDTH - MEM_WIDTH:]
        live[k]["qm"] = (qm * _head_rms(qm) * gqm_ref[...]).astype(BF16)
        c0 = QK_WIDTH + KV_WIDTH
        gate_b = proj[:, c0:c0 + CONV_WIDTH]
        z = proj[:, c0 + CONV_WIDTH:c0 + 2 * CONV_WIDTH] * proj[:, c0 + 2 * CONV_WIDTH:c0 + 3 * CONV_WIDTH]
        zbuf[k, CONV_HALO:CONV_HALO + ts, :] = z
        z1 = zbuf[k, CONV_HALO - 1:CONV_HALO - 1 + ts, :]
        z2 = zbuf[k, CONV_HALO - 2:CONV_HALO - 2 + ts, :]
        cw = convw_ref[...]
        live[k]["y_conv"] = gate_b * (cw[0:1] * z2 + cw[1:2] * z1 + cw[2:3] * z)
        zbuf[k, 0:CONV_HALO, :] = zbuf[k, ts:ts + CONV_HALO, :]
        del live[k]["proj"]

    def scores(k):
        q, qm = live[k].pop("q"), live[k].pop("qm")
        for n_blk, (j, kh, half) in enumerate(blocks):
            r0 = j * PAIR
            c0 = kh * SWA_GROUP * HEAD_DIM
            qg = jnp.concatenate([q[r0:r0 + PAIR, c0:c0 + LANES],
                                  q[r0:r0 + PAIR, c0 + LANES:c0 + 2 * LANES]], axis=0)
            sbuf[k, n_blk] = lax.dot_general(qg, kbuf[k, 2 * kh + half, r0:r0 + WIN, :], NT_DIMS,
                                             preferred_element_type=F32)
        kbuf[k, :, 0:HALO, :] = kbuf[k, :, ts:ts + HALO, :]
        for n_blk, (pr, rb) in enumerate(mem_blocks):
            qp = qm[rb * mem_rows:(rb + 1) * mem_rows, pr * LANES:(pr + 1) * LANES]
            for half, mk_ref in enumerate((mka_ref, mkb_ref)):
                smem[k, 2 * n_blk + half] = lax.dot_general(
                    qp, mk_ref[k, :, pr * LANES:(pr + 1) * LANES], NT_DIMS, preferred_element_type=F32)

    def softmaxes(k):
        for n_blk, (j, kh, half) in enumerate(blocks):
            for pr in range(2):
                hd = kh * SWA_GROUP + 2 * pr + half
                sg = sbuf[k, n_blk, pr * PAIR:(pr + 1) * PAIR, :] + bias_ref[hd]
                if j == 0:
                    sg = jnp.where(first_keys_valid, sg, -jnp.inf)
                pbuf[k, n_blk, pr * PAIR:(pr + 1) * PAIR, :] = _softmax_rows(sg, sinks_ref[hd])
        for n_blk in range(2 * len(mem_blocks)):
            pmem[k, n_blk] = _softmax_rows(smem[k, n_blk])

    def weighted_values(k):
        for j in range(ts // PAIR):
            r0 = j * PAIR
            for kh in range(SWA_KV_HEADS):
                c0 = kh * SWA_GROUP * HEAD_DIM
                n_blk = (j * SWA_KV_HEADS + kh) * 2
                o = (jnp.dot(pbuf[k, n_blk], vbuf[k, 2 * kh, r0:r0 + WIN, :], preferred_element_type=F32)
                     + jnp.dot(pbuf[k, n_blk + 1], vbuf[k, 2 * kh + 1, r0:r0 + WIN, :],
                               preferred_element_type=F32))
                yswa[k, r0:r0 + PAIR, c0:c0 + LANES] = o[0:PAIR]
                yswa[k, r0:r0 + PAIR, c0 + LANES:c0 + 2 * LANES] = o[PAIR:]
        vbuf[k, :, 0:HALO, :] = vbuf[k, :, ts:ts + HALO, :]
        for n_blk, (pr, rb) in enumerate(mem_blocks):
            cols = slice(pr * LANES, (pr + 1) * LANES)
            ymem[k, rb * mem_rows:(rb + 1) * mem_rows, cols] = (
                jnp.dot(pmem[k, 2 * n_blk], mva_ref[k, :, cols], preferred_element_type=F32)
                + jnp.dot(pmem[k, 2 * n_blk + 1], mvb_ref[k, :, cols], preferred_element_type=F32))

    def project_out(k):
        gout = gout_ref[...]
        ys, ym, y_conv = yswa[k], ymem[k], live[k].pop("y_conv")
        a = (ys * _rms_scale(ys, SWA_WIDTH) * gout[:, :SWA_WIDTH]).astype(BF16)
        b = (y_conv * _rms_scale(y_conv, CONV_WIDTH) * gout[:, SWA_WIDTH:SWA_WIDTH + CONV_WIDTH]).astype(BF16)
        c = (ym * _rms_scale(ym, MEM_WIDTH) * gout[:, SWA_WIDTH + CONV_WIDTH:]).astype(BF16)
        out = x_ref[k] + jnp.dot(a, wout_ref[0:SWA_WIDTH, :], preferred_element_type=F32)
        out = out + jnp.dot(b, wout_ref[SWA_WIDTH:SWA_WIDTH + CONV_WIDTH, :], preferred_element_type=F32)
        out = out + jnp.dot(c, wout_ref[SWA_WIDTH + CONV_WIDTH:, :], preferred_element_type=F32)
        o_ref[k] = out

    phases = (project, prepare, scores, softmaxes, weighted_values, project_out)
    for step in range(len(phases) + MIX_LAG * (nseqs - 1)):
        for k in range(nseqs):
            p = step - MIX_LAG * k
            if 0 <= p < len(phases):
                phases[p](k)


def _mixer(x3d, sinks, gmix, w_in, gqk, gqm, bias, conv_w, mem_kv, gout, w_out):
    batch, seq, _ = x3d.shape
    ts = min(seq, 512)
    nseqs = MIX_SEQS if batch % MIX_SEQS == 0 else 1
    mem_len = mem_kv[0].shape[1]
    mem_rows = min(ts, MEM_ROWS)
    const = lambda b, s, sk: (0, 0)
    mem_spec = pl.BlockSpec((nseqs, mem_len, MEM_WIDTH), lambda b, s, sk: (b, 0, 0))
    grid_spec = pltpu.PrefetchScalarGridSpec(
        num_scalar_prefetch=1,
        grid=(batch // nseqs, seq // ts),
        in_specs=[
            pl.BlockSpec((nseqs, ts, D_MODEL), lambda b, s, sk: (b, s, 0)),
            pl.BlockSpec((1, D_MODEL), const),
            pl.BlockSpec((D_MODEL, IN_WIDTH), const),
            pl.BlockSpec((1, QK_WIDTH), const),
            pl.BlockSpec((1, MEM_WIDTH), const),
            pl.BlockSpec((SWA_HEADS, PAIR, WIN), lambda b, s, sk: (0, 0, 0)),
            pl.BlockSpec((CONV_K, CONV_WIDTH), const),
            mem_spec, mem_spec, mem_spec, mem_spec,
            pl.BlockSpec((1, D_MODEL), const),
            pl.BlockSpec((D_MODEL, D_MODEL), const),
        ],
        out_specs=pl.BlockSpec((nseqs, ts, D_MODEL), lambda b, s, sk: (b, s, 0)),
        scratch_shapes=[
            pltpu.VMEM((nseqs, 2 * SWA_KV_HEADS, ts + HALO, LANES), BF16),
            pltpu.VMEM((nseqs, 2 * SWA_KV_HEADS, ts + HALO, LANES), BF16),
            pltpu.VMEM((nseqs, ts + CONV_HALO, CONV_WIDTH), F32),
            pltpu.VMEM((nseqs, ts, SWA_WIDTH), F32),
            pltpu.VMEM((nseqs, ts, MEM_WIDTH), F32),
            pltpu.VMEM((nseqs, ts // PAIR * 2 * SWA_KV_HEADS, 2 * PAIR, WIN), F32),
            pltpu.VMEM((nseqs, ts // PAIR * 2 * SWA_KV_HEADS, 2 * PAIR, WIN), BF16),
            pltpu.VMEM((nseqs, MEM_HEADS * (ts // mem_rows), mem_rows, mem_len), F32),
            pltpu.VMEM((nseqs, MEM_HEADS * (ts // mem_rows), mem_rows, mem_len), BF16),
        ],
    )
    return pl.pallas_call(
        functools.partial(_mixer_kernel, ts=ts, nseqs=nseqs),
        grid_spec=grid_spec,
        out_shape=jax.ShapeDtypeStruct(x3d.shape, F32),
        compiler_params=pltpu.CompilerParams(
            dimension_semantics=("arbitrary", "arbitrary"), vmem_limit_bytes=VMEM_LIMIT_BYTES),
        name="token_mixer",
    )(sinks, x3d, gmix, w_in, gqk, gqm, bias, conv_w, *mem_kv, gout, w_out)


FF_CHUNK = 256


def _swiglu_accumulate(h, wg_ref, wu_ref, wd_ref, acc_ref):
    tf = wg_ref.shape[-1]
    for c in range(tf // FF_CHUNK):
        sl = slice(c * FF_CHUNK, (c + 1) * FF_CHUNK)
        gate = jnp.dot(h, wg_ref[:, sl], preferred_element_type=F32)
        up = jnp.dot(h, wu_ref[:, sl], preferred_element_type=F32)
        act = (gate * jax.nn.sigmoid(gate) * up).astype(BF16)
        acc_ref[...] += jnp.dot(act, wd_ref[sl, :], preferred_element_type=F32)


def _dense_ffn_kernel(x_ref, g_ref, wg_ref, wu_ref, wd_ref, cg_ref, cu_ref, cd_ref,
                      o_ref, og_ref, ou_ref, od_ref):
    og_ref[...] = cg_ref[...].astype(BF16)
    ou_ref[...] = cu_ref[...].astype(BF16)
    od_ref[...] = cd_ref[...].astype(BF16)
    x = x_ref[...]
    h = (x * _rms_scale(x, D_MODEL) * g_ref[...]).astype(BF16)
    o_ref[...] = x
    _swiglu_accumulate(h, wg_ref, wu_ref, wd_ref, o_ref)


def _dense_ffn(x2d, g, layer, wg, wu, wd, ex_gate, ex_up, ex_down):
    n = x2d.shape[0]
    tm = min(n, 512)
    steps = n // tm
    rows_gu = ex_gate.shape[1] // steps
    rows_d = ex_down.shape[1] // steps
    resident = functools.partial(pl.BlockSpec, pipeline_mode=pl.Buffered(1))
    cast_gu_in = pl.BlockSpec((None, rows_gu, D_FF), lambda i: (layer, i, 0))
    cast_gu_out = pl.BlockSpec((rows_gu, D_FF), lambda i: (i, 0))
    return pl.pallas_call(
        _dense_ffn_kernel,
        grid=(steps,),
        in_specs=[
            pl.BlockSpec((tm, D_MODEL), lambda i: (i, 0)),
            pl.BlockSpec((1, D_MODEL), lambda i: (0, 0)),
            resident((None, D_MODEL, D_FF), lambda i: (layer, 0, 0)),
            resident((None, D_MODEL, D_FF), lambda i: (layer, 0, 0)),
            resident((None, D_FF, D_MODEL), lambda i: (layer, 0, 0)),
            cast_gu_in,
            cast_gu_in,
            pl.BlockSpec((None, rows_d, D_MODEL), lambda i: (layer, i, 0)),
        ],
        out_specs=[
            pl.BlockSpec((tm, D_MODEL), lambda i: (i, 0)),
            cast_gu_out,
            cast_gu_out,
            pl.BlockSpec((rows_d, D_MODEL), lambda i: (i, 0)),
        ],
        out_shape=[jax.ShapeDtypeStruct(x2d.shape, F32),
                   jax.ShapeDtypeStruct(ex_gate.shape[1:], BF16),
                   jax.ShapeDtypeStruct(ex_up.shape[1:], BF16),
                   jax.ShapeDtypeStruct(ex_down.shape[1:], BF16)],
        compiler_params=pltpu.CompilerParams(
            dimension_semantics=("arbitrary",), vmem_limit_bytes=VMEM_LIMIT_BYTES),
        name="dense_ffn",
    )(x2d, g, wg, wu, wd, ex_gate, ex_up, ex_down)


MOE_TT = 256
MOE_TM = 512
ROW_ALIGN = 16
MOE_WIN = MOE_TT + ROW_ALIGN
DISPATCH_FAST_WIN = 144
DISPATCH_FAST_FILL = DISPATCH_FAST_WIN - ROW_ALIGN
COMBINE_FAST_WIN = LANES
REGION_SLACK = MOE_WIN
ROUTE_ROWS = 8
ROUTER_ROWS = 1024


def _moe_ffn_kernel(te_ref, tb_ref, nu_ref, x_ref, wg_ref, wu_ref, wd_ref, ys_hbm, o_ref, acc_ref):
    del te_ref, tb_ref, ys_hbm
    f = pl.program_id(1)
    used = pl.program_id(0) < nu_ref[0]

    @pl.when(jnp.logical_and(used, f == 0))
    def _():
        acc_ref[...] = jnp.zeros(acc_ref.shape, F32)

    @pl.when(used)
    def _():
        _swiglu_accumulate(x_ref[...], wg_ref, wu_ref, wd_ref, acc_ref)

    @pl.when(jnp.logical_and(used, f == pl.num_programs(1) - 1))
    def _():
        o_ref[...] = acc_ref[...].astype(BF16)


def _moe_ffn(xs, tile_expert, tile_block, n_used, wg, wu, wd, ys):
    p = xs.shape[0]
    tm = MOE_TM
    tf = D_FF
    nf = D_FF // tf
    max_tiles = tile_expert.shape[0]

    def f_eff(i, f, nu):
        return jnp.where(i < nu[0], f, nf - 1)

    grid_spec = pltpu.PrefetchScalarGridSpec(
        num_scalar_prefetch=3,
        grid=(max_tiles, nf),
        in_specs=[
            pl.BlockSpec((tm, D_MODEL), lambda i, f, te, tb, nu: (tb[i], 0)),
            pl.BlockSpec((None, D_MODEL, tf), lambda i, f, te, tb, nu: (te[i], 0, f_eff(i, f, nu))),
            pl.BlockSpec((None, D_MODEL, tf), lambda i, f, te, tb, nu: (te[i], 0, f_eff(i, f, nu))),
            pl.BlockSpec((None, tf, D_MODEL), lambda i, f, te, tb, nu: (te[i], f_eff(i, f, nu), 0)),
            pl.BlockSpec(memory_space=pl.ANY),
        ],
        out_specs=pl.BlockSpec((tm, D_MODEL), lambda i, f, te, tb, nu: (tb[i], 0)),
        scratch_shapes=[pltpu.VMEM((tm, D_MODEL), F32)],
    )
    return pl.pallas_call(
        _moe_ffn_kernel,
        grid_spec=grid_spec,
        out_shape=jax.ShapeDtypeStruct((p, D_MODEL), BF16),
        input_output_aliases={7: 0},
        compiler_params=pltpu.CompilerParams(
            dimension_semantics=("arbitrary", "arbitrary"), vmem_limit_bytes=VMEM_LIMIT_BYTES),
        name="moe_ffn",
    )(tile_expert, tile_block, n_used, xs, wg, wu, wd, ys)


def _router_kernel(x_ref, g_ref, wh_ref, wl_ref, b_ref, ltri_ref, hn_ref, rc_ref, rt_ref, cnt_ref, xs0_ref, ys0_ref):
    xs0_ref[...] = jnp.zeros(xs0_ref.shape, BF16)
    ys0_ref[...] = jnp.zeros(ys0_ref.shape, BF16)
    x = x_ref[...]
    h = x * _rms_scale(x, D_MODEL) * g_ref[...]
    hh = h.astype(BF16)
    hn_ref[...] = hh
    hl = (h - hh.astype(F32)).astype(BF16)
    wh = wh_ref[...]
    logits = (jnp.dot(hh, wh, preferred_element_type=F32)
              + jnp.dot(hl, wh, preferred_element_type=F32)
              + jnp.dot(hh, wl_ref[...], preferred_element_type=F32)) + b_ref[...]
    lane = lax.broadcasted_iota(jnp.int32, (MOE_TT, ROUTER_PAD), 1)
    for blk in range(x.shape[0] // MOE_TT):
        rows = slice(blk * MOE_TT, (blk + 1) * MOE_TT)
        lg = logits[rows]
        v1 = jnp.max(lg, axis=-1, keepdims=True)
        i1 = jnp.min(jnp.where(lg == v1, lane, ROUTER_PAD), axis=-1, keepdims=True)
        rest = jnp.where(lane == i1, -jnp.inf, lg)
        v2 = jnp.max(rest, axis=-1, keepdims=True)
        i2 = jnp.min(jnp.where(rest == v2, lane, ROUTER_PAD), axis=-1, keepdims=True)
        e2 = jnp.exp(v2 - v1)
        den = 1.0 + e2
        chosen = jnp.logical_or(lane == i1, lane == i2)
        cum = jnp.dot(ltri_ref[...], jnp.where(chosen, 1.0, 0.0).astype(BF16), preferred_element_type=F32)
        r1 = jnp.sum(jnp.where(lane == i1, cum, 0.0), axis=-1, keepdims=True) - 1.0
        r2 = jnp.sum(jnp.where(lane == i2, cum, 0.0), axis=-1, keepdims=True) - 1.0
        rc = jnp.zeros(lg.shape, F32)
        for k, col in enumerate((i1.astype(F32), i2.astype(F32), r1, r2, 1.0 / den, e2 / den)):
            rc = jnp.where(lane == k, col, rc)
        rc_ref[rows, :] = rc
        rt_ref[:, rows] = rc.T[:ROUTE_ROWS, :]
        cnt_ref[blk] = cum[MOE_TT - 1:MOE_TT, :].astype(jnp.int32)


def _router(x2d, g, w_hi, w_lo, b_pad, ltri):
    n = x2d.shape[0]
    tr = min(n, ROUTER_ROWS)
    tiles = tr // MOE_TT
    sorted_rows = _sorted_rows(n)
    zero_rows = sorted_rows // (n // tr)
    return pl.pallas_call(
        _router_kernel,
        grid=(n // tr,),
        in_specs=[
            pl.BlockSpec((tr, D_MODEL), lambda i: (i, 0)),
            pl.BlockSpec((1, D_MODEL), lambda i: (0, 0)),
            pl.BlockSpec((D_MODEL, ROUTER_PAD), lambda i: (0, 0)),
            pl.BlockSpec((D_MODEL, ROUTER_PAD), lambda i: (0, 0)),
            pl.BlockSpec((1, ROUTER_PAD), lambda i: (0, 0)),
            pl.BlockSpec((MOE_TT, MOE_TT), lambda i: (0, 0)),
        ],
        out_specs=[
            pl.BlockSpec((tr, D_MODEL), lambda i: (i, 0)),
            pl.BlockSpec((tr, ROUTER_PAD), lambda i: (i, 0)),
            pl.BlockSpec((ROUTE_ROWS, tr), lambda i: (0, i)),
            pl.BlockSpec((tiles, 1, ROUTER_PAD), lambda i: (i, 0, 0)),
            pl.BlockSpec((zero_rows, D_MODEL), lambda i: (i, 0)),
            pl.BlockSpec((zero_rows, D_MODEL), lambda i: (i, 0)),
        ],
        out_shape=[jax.ShapeDtypeStruct((n, D_MODEL), BF16),
                   jax.ShapeDtypeStruct((n, ROUTER_PAD), F32),
                   jax.ShapeDtypeStruct((ROUTE_ROWS, n), F32),
                   jax.ShapeDtypeStruct((n // MOE_TT, 1, ROUTER_PAD), jnp.int32),
                   jax.ShapeDtypeStruct((sorted_rows, D_MODEL), BF16),
                   jax.ShapeDtypeStruct((sorted_rows, D_MODEL), BF16)],
        compiler_params=pltpu.CompilerParams(
            dimension_semantics=("arbitrary",), vmem_limit_bytes=VMEM_LIMIT_BYTES),
        name="router",
    )(x2d, g, w_hi, w_lo, b_pad, ltri)


def _routing_tables(cnt, n):
    i32 = jnp.int32
    nt = n // MOE_TT
    cnt = cnt.reshape(nt, ROUTER_PAD)[:, :N_EXPERTS]
    total = jnp.sum(cnt, axis=0)
    region = ((total + REGION_SLACK + MOE_TM - 1) // MOE_TM) * MOE_TM
    start = jnp.cumsum(region) - region
    first = start[None, :] + jnp.cumsum(cnt, axis=0) - cnt
    a = jnp.concatenate([first, (start + total)[None, :]], axis=0).reshape(-1).astype(i32)
    ntile = (total + MOE_TM - 1) // MOE_TM
    tend = jnp.cumsum(ntile)
    n_used = tend[-1]
    max_tiles = 2 * n // MOE_TM + N_EXPERTS
    tile = jnp.minimum(jnp.arange(max_tiles, dtype=i32), n_used - 1)
    tile_expert = jnp.minimum(jnp.sum((tend[None, :] <= tile[:, None]).astype(i32), axis=1), N_EXPERTS - 1)
    tile_block = start[tile_expert] // MOE_TM + tile - (tend - ntile)[tile_expert]
    fill = jnp.max(jnp.bitwise_and(first, ROW_ALIGN - 1) + cnt, axis=1)
    fast_dispatch = jnp.logical_and(fill <= DISPATCH_FAST_FILL, jnp.arange(nt) < nt - 1).astype(i32)
    fast_combine = (fill <= COMBINE_FAST_WIN).astype(i32)
    return (a, tile_expert.astype(i32), tile_block.astype(i32), n_used.reshape(1).astype(i32),
            fast_dispatch, fast_combine)


def _sorted_rows(n):
    return -(-(2 * n + N_EXPERTS * (REGION_SLACK + MOE_TM)) // MOE_TM) * MOE_TM


def _window(a_ref, t, e):
    a = a_ref[t * N_EXPERTS + e]
    off = jnp.bitwise_and(a, ROW_ALIGN - 1)
    return pl.multiple_of(a - off, ROW_ALIGN), off


def _expert_slot(e, e1, e2, r1, r2, off):
    d = jnp.where(e1 == e, r1, jnp.where(e2 == e, r2, -1.0))
    return jnp.where(d >= 0.0, d + off.astype(F32), -1.0).astype(jnp.int32)


def _dispatch_kernel(a_ref, fast_ref, hn_ref, rt_ref, zeros_hbm, xs_hbm, carry, stage, sem):
    del zeros_hbm
    t = pl.program_id(0)
    nt = pl.num_programs(0)
    slot = t % 2

    def window_copy(sl, e, base, win):
        return pltpu.make_async_copy(stage.at[sl, e, 0:win], xs_hbm.at[pl.ds(base, win)], sem)

    @pl.when(t == 0)
    def _():
        carry[...] = jnp.zeros(carry.shape, F32)

    def stage_and_send(win, prev_win_is_fast):
        hn = hn_ref[...]
        rt = rt_ref[...]
        e1, e2, r1, r2 = rt[0:1], rt[1:2], rt[2:3], rt[3:4]
        row = lax.broadcasted_iota(jnp.int32, (win, MOE_TT), 0)
        head_row = lax.broadcasted_iota(jnp.int32, (ROW_ALIGN, 1), 0)
        bases = []
        for e in range(N_EXPERTS):
            base, off = _window(a_ref, t, e)
            bases.append(base)
            slot_of = _expert_slot(e, e1, e2, r1, r2, off)
            sel = jnp.where(row == slot_of, 1.0, 0.0).astype(BF16)
            rows = jnp.dot(sel, hn, preferred_element_type=F32)
            head = jnp.where(head_row < off, carry[e], rows[0:ROW_ALIGN])
            stage[slot, e, 0:ROW_ALIGN, :] = head.astype(BF16)
            stage[slot, e, ROW_ALIGN:win, :] = rows[ROW_ALIGN:].astype(BF16)
            filled = off + a_ref[(t + 1) * N_EXPERTS + e] - a_ref[t * N_EXPERTS + e]
            last_group = pl.multiple_of(lax.shift_right_logical(filled, 4) * ROW_ALIGN, ROW_ALIGN)
            carry[e] = stage[slot, e, pl.ds(last_group, ROW_ALIGN), :].astype(F32)

        for was_fast, prev_win in ((1, DISPATCH_FAST_WIN), (0, MOE_WIN)):
            @pl.when(jnp.logical_and(t > 0, prev_win_is_fast == was_fast))
            def _():
                for e in range(N_EXPERTS):
                    window_copy(1 - slot, e, 0, prev_win).wait()

        for e in range(N_EXPERTS):
            window_copy(slot, e, bases[e], win).start()

    is_fast = fast_ref[t]
    prev_fast = fast_ref[jnp.maximum(t - 1, 0)]

    @pl.when(is_fast == 1)
    def _():
        stage_and_send(DISPATCH_FAST_WIN, prev_fast)

    @pl.when(is_fast == 0)
    def _():
        stage_and_send(MOE_WIN, prev_fast)

    @pl.when(t == nt - 1)
    def _():
        for e in range(N_EXPERTS):
            window_copy(slot, e, 0, MOE_WIN).wait()


def _dispatch(hn, route_t, a, fast, zeros):
    n = hn.shape[0]
    tt = MOE_TT
    grid_spec = pltpu.PrefetchScalarGridSpec(
        num_scalar_prefetch=2,
        grid=(n // tt,),
        in_specs=[
            pl.BlockSpec((tt, D_MODEL), lambda t, a, fs: (t, 0)),
            pl.BlockSpec((ROUTE_ROWS, tt), lambda t, a, fs: (0, t)),
            pl.BlockSpec(memory_space=pl.ANY),
        ],
        out_specs=pl.BlockSpec(memory_space=pl.ANY),
        scratch_shapes=[
            pltpu.VMEM((N_EXPERTS, ROW_ALIGN, D_MODEL), F32),
            pltpu.VMEM((2, N_EXPERTS, MOE_WIN, D_MODEL), BF16),
            pltpu.SemaphoreType.DMA(()),
        ],
    )
    return pl.pallas_call(
        _dispatch_kernel,
        grid_spec=grid_spec,
        out_shape=jax.ShapeDtypeStruct(zeros.shape, BF16),
        input_output_aliases={4: 0},
        compiler_params=pltpu.CompilerParams(
            dimension_semantics=("arbitrary",), vmem_limit_bytes=VMEM_LIMIT_BYTES),
        name="moe_dispatch",
    )(a, fast, hn, route_t, zeros)


def _combine_kernel(a_ref, fast_ref, x_ref, rc_ref, ys_hbm, o_ref, ybuf, yfast, sem):
    t = pl.program_id(0)
    nt = pl.num_programs(0)
    slot = t % 2
    x = x_ref[...]
    rc = rc_ref[...]
    e1, e2, r1, r2, g1, g2 = (rc[:, k:k + 1] for k in range(6))

    def wide_copy(tile, sl, e):
        base, _ = _window(a_ref, tile, e)
        return pltpu.make_async_copy(ys_hbm.at[pl.ds(base, MOE_WIN)], ybuf.at[sl, e], sem.at[sl])

    def narrow_copy(tile, sl, e):
        base, _ = _window(a_ref, tile, e)
        return pltpu.make_async_copy(ys_hbm.at[pl.ds(base, COMBINE_FAST_WIN)],
                                     yfast.at[sl, pl.ds(e * COMBINE_FAST_WIN, COMBINE_FAST_WIN)], sem.at[sl])

    def start_windows(tile, sl):
        for flag, copy in ((1, narrow_copy), (0, wide_copy)):
            @pl.when(fast_ref[tile] == flag)
            def _():
                for e in range(N_EXPERTS):
                    copy(tile, sl, e).start()

    @pl.when(t == 0)
    def _():
        start_windows(0, 0)

    @pl.when(t + 1 < nt)
    def _():
        start_windows(t + 1, 1 - slot)

    @pl.when(fast_ref[t] == 0)
    def _():
        for e in range(N_EXPERTS):
            wide_copy(t, slot, e).wait()
        lane = lax.broadcasted_iota(jnp.int32, (MOE_TT, MOE_WIN), 1)
        acc = jnp.zeros((MOE_TT, D_MODEL), F32)
        for e in range(N_EXPERTS):
            _, off = _window(a_ref, t, e)
            slot_of = _expert_slot(e, e1, e2, r1, r2, off)
            sel = jnp.where(lane == slot_of, 1.0, 0.0).astype(BF16)
            gate = jnp.where(e1 == e, g1, jnp.where(e2 == e, g2, 0.0))
            acc = acc + gate * jnp.dot(sel, ybuf[slot, e], preferred_element_type=F32)
        o_ref[...] = x + acc

    @pl.when(fast_ref[t] == 1)
    def _():
        for e in range(N_EXPERTS):
            narrow_copy(t, slot, e).wait()
        shape = (MOE_TT, COMBINE_FAST_WIN)
        lane = lax.broadcasted_iota(jnp.int32, shape, 1).astype(F32)
        offs = [_window(a_ref, t, e)[1].astype(F32) for e in range(N_EXPERTS)]
        picked = []
        for ek, rk in ((e1, r1), (e2, r2)):
            eb = jnp.broadcast_to(ek, shape)
            pos = jnp.broadcast_to(rk, shape)
            for e in range(N_EXPERTS):
                pos = pos + jnp.where(eb == e, offs[e], 0.0)
            hit = lane == pos
            sel = jnp.concatenate(
                [jnp.where(jnp.logical_and(eb == e, hit), 1.0, 0.0).astype(BF16) for e in range(N_EXPERTS)],
                axis=1)
            picked.append(jnp.dot(sel, yfast[slot], preferred_element_type=F32))
        o_ref[...] = x + (g1 * picked[0] + g2 * picked[1])


def _combine(x2d, route_c, ys, a, fast):
    n = x2d.shape[0]
    tt = MOE_TT
    grid_spec = pltpu.PrefetchScalarGridSpec(
        num_scalar_prefetch=2,
        grid=(n // tt,),
        in_specs=[
            pl.BlockSpec((tt, D_MODEL), lambda t, a, fs: (t, 0)),
            pl.BlockSpec((tt, ROUTER_PAD), lambda t, a, fs: (t, 0)),
            pl.BlockSpec(memory_space=pl.ANY),
        ],
        out_specs=pl.BlockSpec((tt, D_MODEL), lambda t, a, fs: (t, 0)),
        scratch_shapes=[
            pltpu.VMEM((2, N_EXPERTS, MOE_WIN, D_MODEL), BF16),
            pltpu.VMEM((2, N_EXPERTS * COMBINE_FAST_WIN, D_MODEL), BF16),
            pltpu.SemaphoreType.DMA((2,)),
        ],
    )
    return pl.pallas_call(
        _combine_kernel,
        grid_spec=grid_spec,
        out_shape=jax.ShapeDtypeStruct((n, D_MODEL), F32),
        compiler_params=pltpu.CompilerParams(
            dimension_semantics=("arbitrary",), vmem_limit_bytes=VMEM_LIMIT_BYTES),
        name="moe_combine",
    )(a, fast, x2d, route_c, ys)


def _swa_bias():
    qi = jnp.arange(PAIR)[:, None]
    kj = jnp.arange(WIN)[None, :]
    dist = jnp.abs(qi + HALO - kj).astype(F32)
    kc = kj // CHUNK
    qc = qi // CHUNK
    visible = jnp.logical_and(kc >= qc, kc <= qc + 2)
    slopes = jnp.asarray([2.0 ** (-8.0 * (i + 1) / SWA_HEADS) for i in range(SWA_HEADS)], F32)
    bias = -slopes[:, None, None] * dist[None]
    return jnp.where(visible[None], bias, -jnp.inf)


def kernel(x, mem, g_mix, w_in, g_q_swa, g_k_swa, sinks, conv_w, g_mem, w_mem_kv, g_q_mem, g_k_mem,
           g_out_swa, g_out_conv, g_out_mem, w_out, g_ffn, w_gate_dense, w_up_dense, w_down_dense,
           w_router, b_router, w_gate_moe, w_up_moe, w_down_moe):
    batch, seq, _ = x.shape
    depth = g_mix.shape[0]
    n = batch * seq
    assert seq % min(seq, 512) == 0 and min(seq, 512) % PAIR == 0, "sequence length must tile into query pairs"
    assert n % min(n, ROUTER_ROWS) == 0 and n % MOE_TM == 0, "token count must tile the routing kernels"
    scale = HEAD_DIM ** -0.5

    bias = _swa_bias()
    gqk = jnp.concatenate([jnp.tile(g_q_swa * scale, (1, SWA_HEADS)),
                           jnp.tile(g_k_swa, (1, SWA_KV_HEADS))], axis=1)[:, None, :]
    gqm = jnp.tile(g_q_mem * scale, (1, MEM_HEADS))[:, None, :]
    gkm = jnp.tile(g_k_mem, (1, MEM_HEADS))[:, None, :]
    gout = jnp.concatenate([g_out_swa, g_out_conv, g_out_mem], axis=1)[:, None, :]

    w_in_b = w_in.astype(BF16)
    w_out_b = w_out.astype(BF16)
    mem_kv_all = _mem_kv(mem.reshape(-1, D_MODEL), g_mem[:, None, :], w_mem_kv.astype(BF16), gkm)

    wr = jnp.pad(w_router, ((0, 0), (0, 0), (0, ROUTER_PAD - N_EXPERTS)))
    wr_hi = wr.astype(BF16)
    wr_lo = (wr - wr_hi.astype(F32)).astype(BF16)
    br = jnp.pad(b_router, ((0, 0), (0, ROUTER_PAD - N_EXPERTS)), constant_values=-jnp.inf)[:, None, :]
    ltri = (jnp.arange(MOE_TT)[:, None] >= jnp.arange(MOE_TT)[None, :]).astype(BF16)

    wg_dense, wu_dense, wd_dense = (w.astype(BF16) for w in (w_gate_dense, w_up_dense, w_down_dense))
    n_moe, n_exp = w_gate_moe.shape[:2]
    assert depth % 2 == 0 and n_moe == depth // 2, "every expert layer must follow a dense layer"
    ex_gate = w_gate_moe.reshape(n_moe, n_exp * D_MODEL, D_FF)
    ex_up = w_up_moe.reshape(n_moe, n_exp * D_MODEL, D_FF)
    ex_down = w_down_moe.reshape(n_moe, n_exp * D_FF, D_MODEL)

    xs = x.reshape(n, D_MODEL)
    for l in range(depth):
        xs = _mixer(xs.reshape(batch, seq, D_MODEL), sinks[l], g_mix[l][None], w_in_b[l], gqk[l], gqm[l],
                    bias, conv_w[l], [t[l].reshape(batch, -1, MEM_WIDTH) for t in mem_kv_all], gout[l],
                    w_out_b[l]).reshape(n, D_MODEL)
        i = l // 2
        gf = g_ffn[l][None]
        if l % 2 == 0:
            xs, wg_moe, wu_moe, wd_moe = _dense_ffn(xs, gf, i, wg_dense, wu_dense, wd_dense, ex_gate, ex_up, ex_down)
            wg_moe = wg_moe.reshape(n_exp, D_MODEL, D_FF)
            wu_moe = wu_moe.reshape(n_exp, D_MODEL, D_FF)
            wd_moe = wd_moe.reshape(n_exp, D_FF, D_MODEL)
        else:
            hn, route_c, route_t, cnt, xs_zeros, ys_zeros = _router(xs, gf, wr_hi[i], wr_lo[i], br[i], ltri)
            a, tile_expert, tile_block, n_used, fast_d, fast_c = _routing_tables(cnt, n)
            x_sorted = _dispatch(hn, route_t, a, fast_d, xs_zeros)
            y_sorted = _moe_ffn(x_sorted, tile_expert, tile_block, n_used, wg_moe, wu_moe, wd_moe, ys_zeros)
            xs = _combine(xs, route_c, y_sorted, a, fast_c)
    return xs.reshape(batch, seq, D_MODEL)
```

```python
import functools

import jax
import jax.numpy as jnp
from jax import lax
from jax.experimental import pallas as pl
from jax.experimental.pallas import tpu as pltpu

F32 = jnp.float32
BF16 = jnp.bfloat16

D_MODEL = 1024
CHUNK = 64
HEAD_DIM = 64
SWA_HEADS = 8
SWA_KV_HEADS = 2
SWA_GROUP = SWA_HEADS // SWA_KV_HEADS
CONV_WIDTH = 256
CONV_K = 3
MEM_HEADS = 4
SWA_WIDTH = SWA_HEADS * HEAD_DIM
KV_WIDTH = SWA_KV_HEADS * HEAD_DIM
MEM_WIDTH = MEM_HEADS * HEAD_DIM
QK_WIDTH = SWA_WIDTH + KV_WIDTH
IN_WIDTH = SWA_WIDTH + 2 * KV_WIDTH + 3 * CONV_WIDTH + MEM_WIDTH
D_FF = 3584
N_EXPERTS = 8
EPS = 1e-6

PAIR = 2 * CHUNK
WIN = 4 * CHUNK
HALO = WIN - PAIR
CONV_HALO = 8
LANES = 128
ROUTER_PAD = LANES

VMEM_LIMIT_BYTES = 56 * 1024 * 1024

NT_DIMS = (((1,), (1,)), ((), ()))
LOG2E = 1.4426950408889634


def _rms_scale(x, width):
    return lax.rsqrt(jnp.sum(x * x, axis=-1, keepdims=True) * (1.0 / width) + EPS)


def _head_rms(t):
    sq = t * t
    low = lax.broadcasted_iota(jnp.int32, (t.shape[0], LANES), 1) < HEAD_DIM
    scales = []
    for c in range(0, t.shape[1], LANES):
        pair = sq[:, c:c + LANES]
        even = jnp.sum(jnp.where(low, pair, 0.0), axis=-1, keepdims=True)
        odd = jnp.sum(jnp.where(low, 0.0, pair), axis=-1, keepdims=True)
        scales.append(lax.rsqrt(jnp.where(low, even, odd) * (1.0 / HEAD_DIM) + EPS))
    return jnp.concatenate(scales, axis=1)


def _low_half(shape):
    return lax.broadcasted_iota(jnp.int32, shape, len(shape) - 1) % LANES < HEAD_DIM


def _softmax_rows(s, sink=None):
    m = jnp.max(s, axis=-1, keepdims=True)
    if sink is not None:
        m = jnp.maximum(m, sink)
    p = jnp.exp2(s - m)
    den = jnp.sum(p, axis=-1, keepdims=True)
    if sink is not None:
        den = den + jnp.exp2(sink - m)
    return (p * (1.0 / den)).astype(BF16)


def _memkv_kernel(mem_ref, g_ref, w_ref, gk_ref, mka_ref, mkb_ref, mva_ref, mvb_ref):
    m = mem_ref[...]
    hm = (m * _rms_scale(m, D_MODEL) * g_ref[...]).astype(BF16)
    kv = jnp.dot(hm, w_ref[...], preferred_element_type=F32)
    k = kv[:, :MEM_WIDTH]
    k = k * _head_rms(k) * gk_ref[...]
    v = kv[:, MEM_WIDTH:]
    low = _low_half(k.shape)
    mka_ref[...] = jnp.where(low, k, 0.0).astype(BF16)
    mkb_ref[...] = jnp.where(low, 0.0, k).astype(BF16)
    mva_ref[...] = jnp.where(low, v, 0.0).astype(BF16)
    mvb_ref[...] = jnp.where(low, 0.0, v).astype(BF16)


def _mem_kv(mem2d, g_mem, w_mem_kv, gk_mem):
    depth = g_mem.shape[0]
    rows = mem2d.shape[0]
    tr = min(rows, 512)
    out = jax.ShapeDtypeStruct((depth, rows, MEM_WIDTH), BF16)
    out_spec = pl.BlockSpec((None, tr, MEM_WIDTH), lambda l, i: (l, i, 0))
    return pl.pallas_call(
        _memkv_kernel,
        grid=(depth, rows // tr),
        in_specs=[
            pl.BlockSpec((tr, D_MODEL), lambda l, i: (i, 0)),
            pl.BlockSpec((None, 1, D_MODEL), lambda l, i: (l, 0, 0)),
            pl.BlockSpec((None, D_MODEL, 2 * MEM_WIDTH), lambda l, i: (l, 0, 0)),
            pl.BlockSpec((None, 1, MEM_WIDTH), lambda l, i: (l, 0, 0)),
        ],
        out_specs=[out_spec] * 4,
        out_shape=[out] * 4,
        compiler_params=pltpu.CompilerParams(
            dimension_semantics=("arbitrary", "arbitrary"), vmem_limit_bytes=VMEM_LIMIT_BYTES),
        name="mem_kv",
    )(mem2d, g_mem, w_mem_kv, gk_mem)


MEM_ROWS = 256
MIX_SEQS = 2
MIX_LAG = 2


def _mixer_kernel(sinks_ref, x_ref, gmix_ref, win_ref, gqk_ref, gqm_ref, bias_ref,
                  convw_ref, mka_ref, mkb_ref, mva_ref, mvb_ref, gout_ref, wout_ref, o_ref,
                  kbuf, vbuf, zbuf, yswa, ymem, sbuf, pbuf, smem, pmem, *, ts, nseqs):
    s_idx = pl.program_id(1)
    mem_rows = min(ts, MEM_ROWS)
    mem_blocks = [(pr, rb) for pr in range(MEM_HEADS // 2) for rb in range(ts // mem_rows)]
    blocks = [(j, kh, half) for j in range(ts // PAIR) for kh in range(SWA_KV_HEADS) for half in range(2)]
    key_lane = lax.broadcasted_iota(jnp.int32, (1, WIN), 1)
    first_keys_valid = jnp.logical_or(key_lane >= HALO, s_idx > 0)
    live = [dict() for _ in range(nseqs)]

    @pl.when(s_idx == 0)
    def _():
        kbuf[:, :, 0:HALO, :] = jnp.zeros((nseqs, 2 * SWA_KV_HEADS, HALO, LANES), BF16)
        vbuf[:, :, 0:HALO, :] = jnp.zeros((nseqs, 2 * SWA_KV_HEADS, HALO, LANES), BF16)
        zbuf[:, 0:CONV_HALO, :] = jnp.zeros((nseqs, CONV_HALO, CONV_WIDTH), F32)

    def project(k):
        x = x_ref[k]
        h = (x * _rms_scale(x, D_MODEL) * gmix_ref[...]).astype(BF16)
        live[k]["proj"] = jnp.dot(h, win_ref[...], preferred_element_type=F32)

    def prepare(k):
        proj = live[k]["proj"]
        qk = proj[:, :QK_WIDTH]
        qk = qk * _head_rms(qk) * gqk_ref[...]
        live[k]["q"] = qk[:, :SWA_WIDTH].astype(BF16)
        low = _low_half((ts, KV_WIDTH))
        for buf, t in ((kbuf, qk[:, SWA_WIDTH:]), (vbuf, proj[:, QK_WIDTH:QK_WIDTH + KV_WIDTH])):
            swapped = pltpu.roll(t, HEAD_DIM, 1)
            buf[k, 0, HALO:HALO + ts, :] = jnp.where(low, t, 0.0).astype(BF16)
            buf[k, 1, HALO:HALO + ts, :] = jnp.where(low, 0.0, swapped).astype(BF16)
            buf[k, 2, HALO:HALO + ts, :] = jnp.where(low, swapped, 0.0).astype(BF16)
            buf[k, 3, HALO:HALO + ts, :] = jnp.where(low, 0.0, t).astype(BF16)
        qm = proj[:, IN_WIDTH - MEM_WIDTH:]
        live[k]["qm"] = (qm * _head_rms(qm) * gqm_ref[...]).astype(BF16)
        c0 = QK_WIDTH + KV_WIDTH
        gate_b = proj[:, c0:c0 + CONV_WIDTH]
        z = proj[:, c0 + CONV_WIDTH:c0 + 2 * CONV_WIDTH] * proj[:, c0 + 2 * CONV_WIDTH:c0 + 3 * CONV_WIDTH]
        zbuf[k, CONV_HALO:CONV_HALO + ts, :] = z
        z1 = zbuf[k, CONV_HALO - 1:CONV_HALO - 1 + ts, :]
        z2 = zbuf[k, CONV_HALO - 2:CONV_HALO - 2 + ts, :]
        cw = convw_ref[...]
        live[k]["y_conv"] = gate_b * (cw[0:1] * z2 + cw[1:2] * z1 + cw[2:3] * z)
        zbuf[k, 0:CONV_HALO, :] = zbuf[k, ts:ts + CONV_HALO, :]
        del live[k]["proj"]

    def scores(k):
        q, qm = live[k].pop("q"), live[k].pop("qm")
        for n_blk, (j, kh, half) in enumerate(blocks):
            r0 = j * PAIR
            c0 = kh * SWA_GROUP * HEAD_DIM
            qg = jnp.concatenate([q[r0:r0 + PAIR, c0:c0 + LANES],
                                  q[r0:r0 + PAIR, c0 + LANES:c0 + 2 * LANES]], axis=0)
            sbuf[k, n_blk] = lax.dot_general(qg, kbuf[k, 2 * kh + half, r0:r0 + WIN, :], NT_DIMS,
                                             preferred_element_type=F32)
        kbuf[k, :, 0:HALO, :] = kbuf[k, :, ts:ts + HALO, :]
        for n_blk, (pr, rb) in enumerate(mem_blocks):
            qp = qm[rb * mem_rows:(rb + 1) * mem_rows, pr * LANES:(pr + 1) * LANES]
            for half, mk_ref in enumerate((mka_ref, mkb_ref)):
                smem[k, 2 * n_blk + half] = lax.dot_general(
                    qp, mk_ref[k, :, pr * LANES:(pr + 1) * LANES], NT_DIMS, preferred_element_type=F32)

    def softmaxes(k):
        for n_blk, (j, kh, half) in enumerate(blocks):
            for pr in range(2):
                hd = kh * SWA_GROUP + 2 * pr + half
                sg = sbuf[k, n_blk, pr * PAIR:(pr + 1) * PAIR, :] + bias_ref[hd]
                if j == 0:
                    sg = jnp.where(first_keys_valid, sg, -jnp.inf)
                pbuf[k, n_blk, pr * PAIR:(pr + 1) * PAIR, :] = _softmax_rows(sg, sinks_ref[hd])
        for n_blk in range(2 * len(mem_blocks)):
            pmem[k, n_blk] = _softmax_rows(smem[k, n_blk])

    def weighted_values(k):
        for j in range(ts // PAIR):
            r0 = j * PAIR
            for kh in range(SWA_KV_HEADS):
                c0 = kh * SWA_GROUP * HEAD_DIM
                n_blk = (j * SWA_KV_HEADS + kh) * 2
                o = (jnp.dot(pbuf[k, n_blk], vbuf[k, 2 * kh, r0:r0 + WIN, :], preferred_element_type=F32)
                     + jnp.dot(pbuf[k, n_blk + 1], vbuf[k, 2 * kh + 1, r0:r0 + WIN, :],
                               preferred_element_type=F32))
                yswa[k, r0:r0 + PAIR, c0:c0 + LANES] = o[0:PAIR]
                yswa[k, r0:r0 + PAIR, c0 + LANES:c0 + 2 * LANES] = o[PAIR:]
        vbuf[k, :, 0:HALO, :] = vbuf[k, :, ts:ts + HALO, :]
        for n_blk, (pr, rb) in enumerate(mem_blocks):
            cols = slice(pr * LANES, (pr + 1) * LANES)
            ymem[k, rb * mem_rows:(rb + 1) * mem_rows, cols] = (
                jnp.dot(pmem[k, 2 * n_blk], mva_ref[k, :, cols], preferred_element_type=F32)
                + jnp.dot(pmem[k, 2 * n_blk + 1], mvb_ref[k, :, cols], preferred_element_type=F32))

    def project_out(k):
        gout = gout_ref[...]
        ys, ym, y_conv = yswa[k], ymem[k], live[k].pop("y_conv")
        a = (ys * _rms_scale(ys, SWA_WIDTH) * gout[:, :SWA_WIDTH]).astype(BF16)
        b = (y_conv * _rms_scale(y_conv, CONV_WIDTH) * gout[:, SWA_WIDTH:SWA_WIDTH + CONV_WIDTH]).astype(BF16)
        c = (ym * _rms_scale(ym, MEM_WIDTH) * gout[:, SWA_WIDTH + CONV_WIDTH:]).astype(BF16)
        out = x_ref[k] + jnp.dot(a, wout_ref[0:SWA_WIDTH, :], preferred_element_type=F32)
        out = out + jnp.dot(b, wout_ref[SWA_WIDTH:SWA_WIDTH + CONV_WIDTH, :], preferred_element_type=F32)
        out = out + jnp.dot(c, wout_ref[SWA_WIDTH + CONV_WIDTH:, :], preferred_element_type=F32)
        o_ref[k] = out

    phases = (project, prepare, scores, softmaxes, weighted_values, project_out)
    for step in range(len(phases) + MIX_LAG * (nseqs - 1)):
        for k in range(nseqs):
            p = step - MIX_LAG * k
            if 0 <= p < len(phases):
                phases[p](k)


def _mixer(x3d, sinks, gmix, w_in, gqk, gqm, bias, conv_w, mem_kv, gout, w_out):
    batch, seq, _ = x3d.shape
    ts = min(seq, 512)
    nseqs = MIX_SEQS if batch % MIX_SEQS == 0 else 1
    mem_len = mem_kv[0].shape[1]
    mem_rows = min(ts, MEM_ROWS)
    const = lambda b, s, sk: (0, 0)
    mem_spec = pl.BlockSpec((nseqs, mem_len, MEM_WIDTH), lambda b, s, sk: (b, 0, 0))
    grid_spec = pltpu.PrefetchScalarGridSpec(
        num_scalar_prefetch=1,
        grid=(batch // nseqs, seq // ts),
        in_specs=[
            pl.BlockSpec((nseqs, ts, D_MODEL), lambda b, s, sk: (b, s, 0)),
            pl.BlockSpec((1, D_MODEL), const),
            pl.BlockSpec((D_MODEL, IN_WIDTH), const),
            pl.BlockSpec((1, QK_WIDTH), const),
            pl.BlockSpec((1, MEM_WIDTH), const),
            pl.BlockSpec((SWA_HEADS, PAIR, WIN), lambda b, s, sk: (0, 0, 0)),
            pl.BlockSpec((CONV_K, CONV_WIDTH), const),
            mem_spec, mem_spec, mem_spec, mem_spec,
            pl.BlockSpec((1, D_MODEL), const),
            pl.BlockSpec((D_MODEL, D_MODEL), const),
        ],
        out_specs=pl.BlockSpec((nseqs, ts, D_MODEL), lambda b, s, sk: (b, s, 0)),
        scratch_shapes=[
            pltpu.VMEM((nseqs, 2 * SWA_KV_HEADS, ts + HALO, LANES), BF16),
            pltpu.VMEM((nseqs, 2 * SWA_KV_HEADS, ts + HALO, LANES), BF16),
            pltpu.VMEM((nseqs, ts + CONV_HALO, CONV_WIDTH), F32),
            pltpu.VMEM((nseqs, ts, SWA_WIDTH), F32),
            pltpu.VMEM((nseqs, ts, MEM_WIDTH), F32),
            pltpu.VMEM((nseqs, ts // PAIR * 2 * SWA_KV_HEADS, 2 * PAIR, WIN), F32),
            pltpu.VMEM((nseqs, ts // PAIR * 2 * SWA_KV_HEADS, 2 * PAIR, WIN), BF16),
            pltpu.VMEM((nseqs, MEM_HEADS * (ts // mem_rows), mem_rows, mem_len), F32),
            pltpu.VMEM((nseqs, MEM_HEADS * (ts // mem_rows), mem_rows, mem_len), BF16),
        ],
    )
    return pl.pallas_call(
        functools.partial(_mixer_kernel, ts=ts, nseqs=nseqs),
        grid_spec=grid_spec,
        out_shape=jax.ShapeDtypeStruct(x3d.shape, F32),
        compiler_params=pltpu.CompilerParams(
            dimension_semantics=("arbitrary", "arbitrary"), vmem_limit_bytes=VMEM_LIMIT_BYTES),
        name="token_mixer",
    )(sinks, x3d, gmix, w_in, gqk, gqm, bias, conv_w, *mem_kv, gout, w_out)


FF_CHUNK = 256


def _swiglu_accumulate(h, wg_ref, wu_ref, wd_ref, acc_ref):
    tf = wg_ref.shape[-1]
    for c in range(tf // FF_CHUNK):
        sl = slice(c * FF_CHUNK, (c + 1) * FF_CHUNK)
        gate = jnp.dot(h, wg_ref[:, sl], preferred_element_type=F32)
        up = jnp.dot(h, wu_ref[:, sl], preferred_element_type=F32)
        act = (gate * jax.nn.sigmoid(gate) * up).astype(BF16)
        acc_ref[...] += jnp.dot(act, wd_ref[sl, :], preferred_element_type=F32)


def _dense_ffn_kernel(x_ref, g_ref, wg_ref, wu_ref, wd_ref, cg_ref, cu_ref, cd_ref,
                      o_ref, og_ref, ou_ref, od_ref):
    og_ref[...] = cg_ref[...].astype(BF16)
    ou_ref[...] = cu_ref[...].astype(BF16)
    od_ref[...] = cd_ref[...].astype(BF16)
    x = x_ref[...]
    h = (x * _rms_scale(x, D_MODEL) * g_ref[...]).astype(BF16)
    o_ref[...] = x
    _swiglu_accumulate(h, wg_ref, wu_ref, wd_ref, o_ref)


def _dense_ffn(x2d, g, layer, wg, wu, wd, ex_gate, ex_up, ex_down):
    n = x2d.shape[0]
    tm = min(n, 512)
    steps = n // tm
    rows_gu = ex_gate.shape[1] // steps
    rows_d = ex_down.shape[1] // steps
    resident = functools.partial(pl.BlockSpec, pipeline_mode=pl.Buffered(1))
    cast_gu_in = pl.BlockSpec((None, rows_gu, D_FF), lambda i: (layer, i, 0))
    cast_gu_out = pl.BlockSpec((rows_gu, D_FF), lambda i: (i, 0))
    return pl.pallas_call(
        _dense_ffn_kernel,
        grid=(steps,),
        in_specs=[
            pl.BlockSpec((tm, D_MODEL), lambda i: (i, 0)),
            pl.BlockSpec((1, D_MODEL), lambda i: (0, 0)),
            resident((None, D_MODEL, D_FF), lambda i: (layer, 0, 0)),
            resident((None, D_MODEL, D_FF), lambda i: (layer, 0, 0)),
            resident((None, D_FF, D_MODEL), lambda i: (layer, 0, 0)),
            cast_gu_in,
            cast_gu_in,
            pl.BlockSpec((None, rows_d, D_MODEL), lambda i: (layer, i, 0)),
        ],
        out_specs=[
            pl.BlockSpec((tm, D_MODEL), lambda i: (i, 0)),
            cast_gu_out,
            cast_gu_out,
            pl.BlockSpec((rows_d, D_MODEL), lambda i: (i, 0)),
        ],
        out_shape=[jax.ShapeDtypeStruct(x2d.shape, F32),
                   jax.ShapeDtypeStruct(ex_gate.shape[1:], BF16),
                   jax.ShapeDtypeStruct(ex_up.shape[1:], BF16),
                   jax.ShapeDtypeStruct(ex_down.shape[1:], BF16)],
        compiler_params=pltpu.CompilerParams(
            dimension_semantics=("arbitrary",), vmem_limit_bytes=VMEM_LIMIT_BYTES),
        name="dense_ffn",
    )(x2d, g, wg, wu, wd, ex_gate, ex_up, ex_down)


MOE_TT = 256
MOE_TM = 512
ROW_ALIGN = 16
MOE_WIN = MOE_TT + ROW_ALIGN
DISPATCH_FAST_WIN = 144
DISPATCH_FAST_FILL = DISPATCH_FAST_WIN - ROW_ALIGN
COMBINE_FAST_WIN = LANES
REGION_SLACK = MOE_WIN
ROUTE_ROWS = 8
ROUTER_ROWS = 1024


def _moe_ffn_kernel(be_ref, used_ref, x_ref, wg_ref, wu_ref, wd_ref, o_ref, acc_ref):
    del be_ref
    used = used_ref[pl.program_id(0)] == 1

    @pl.when(used)
    def _():
        acc_ref[...] = jnp.zeros(acc_ref.shape, F32)
        _swiglu_accumulate(x_ref[...], wg_ref, wu_ref, wd_ref, acc_ref)
        o_ref[...] = acc_ref[...].astype(BF16)

    @pl.when(jnp.logical_not(used))
    def _():
        o_ref[...] = jnp.zeros(o_ref.shape, BF16)


def _moe_ffn(xs, block_expert, block_used, wg, wu, wd):
    p = xs.shape[0]
    tm = MOE_TM
    grid_spec = pltpu.PrefetchScalarGridSpec(
        num_scalar_prefetch=2,
        grid=(p // tm,),
        in_specs=[
            pl.BlockSpec((tm, D_MODEL), lambda i, be, us: (i, 0)),
            pl.BlockSpec((None, D_MODEL, D_FF), lambda i, be, us: (be[i], 0, 0)),
            pl.BlockSpec((None, D_MODEL, D_FF), lambda i, be, us: (be[i], 0, 0)),
            pl.BlockSpec((None, D_FF, D_MODEL), lambda i, be, us: (be[i], 0, 0)),
        ],
        out_specs=pl.BlockSpec((tm, D_MODEL), lambda i, be, us: (i, 0)),
        scratch_shapes=[pltpu.VMEM((tm, D_MODEL), F32)],
    )
    return pl.pallas_call(
        _moe_ffn_kernel,
        grid_spec=grid_spec,
        out_shape=jax.ShapeDtypeStruct((p, D_MODEL), BF16),
        compiler_params=pltpu.CompilerParams(
            dimension_semantics=("arbitrary",), vmem_limit_bytes=VMEM_LIMIT_BYTES),
        name="moe_ffn",
    )(block_expert, block_used, xs, wg, wu, wd)


def _router_kernel(x_ref, g_ref, wh_ref, wl_ref, b_ref, ltri_ref, hn_ref, rc_ref, rt_ref, cnt_ref, xs0_ref):
    xs0_ref[...] = jnp.zeros(xs0_ref.shape, BF16)
    x = x_ref[...]
    h = x * _rms_scale(x, D_MODEL) * g_ref[...]
    hh = h.astype(BF16)
    hn_ref[...] = hh
    hl = (h - hh.astype(F32)).astype(BF16)
    wh = wh_ref[...]
    logits = (jnp.dot(hh, wh, preferred_element_type=F32)
              + jnp.dot(hl, wh, preferred_element_type=F32)
              + jnp.dot(hh, wl_ref[...], preferred_element_type=F32)) + b_ref[...]
    lane = lax.broadcasted_iota(jnp.int32, (MOE_TT, ROUTER_PAD), 1)
    for blk in range(x.shape[0] // MOE_TT):
        rows = slice(blk * MOE_TT, (blk + 1) * MOE_TT)
        lg = logits[rows]
        v1 = jnp.max(lg, axis=-1, keepdims=True)
        i1 = jnp.min(jnp.where(lg == v1, lane, ROUTER_PAD), axis=-1, keepdims=True)
        rest = jnp.where(lane == i1, -jnp.inf, lg)
        v2 = jnp.max(rest, axis=-1, keepdims=True)
        i2 = jnp.min(jnp.where(rest == v2, lane, ROUTER_PAD), axis=-1, keepdims=True)
        e2 = jnp.exp(v2 - v1)
        den = 1.0 + e2
        chosen = jnp.logical_or(lane == i1, lane == i2)
        cum = jnp.dot(ltri_ref[...], jnp.where(chosen, 1.0, 0.0).astype(BF16), preferred_element_type=F32)
        r1 = jnp.sum(jnp.where(lane == i1, cum, 0.0), axis=-1, keepdims=True) - 1.0
        r2 = jnp.sum(jnp.where(lane == i2, cum, 0.0), axis=-1, keepdims=True) - 1.0
        rc = jnp.zeros(lg.shape, F32)
        for k, col in enumerate((i1.astype(F32), i2.astype(F32), r1, r2, 1.0 / den, e2 / den)):
            rc = jnp.where(lane == k, col, rc)
        rc_ref[rows, :] = rc
        rt_ref[:, rows] = rc.T[:ROUTE_ROWS, :]
        cnt_ref[blk] = cum[MOE_TT - 1:MOE_TT, :].astype(jnp.int32)


def _router(x2d, g, w_hi, w_lo, b_pad, ltri):
    n = x2d.shape[0]
    tr = min(n, ROUTER_ROWS)
    tiles = tr // MOE_TT
    sorted_rows = _sorted_rows(n)
    zero_rows = sorted_rows // (n // tr)
    return pl.pallas_call(
        _router_kernel,
        grid=(n // tr,),
        in_specs=[
            pl.BlockSpec((tr, D_MODEL), lambda i: (i, 0)),
            pl.BlockSpec((1, D_MODEL), lambda i: (0, 0)),
            pl.BlockSpec((D_MODEL, ROUTER_PAD), lambda i: (0, 0)),
            pl.BlockSpec((D_MODEL, ROUTER_PAD), lambda i: (0, 0)),
            pl.BlockSpec((1, ROUTER_PAD), lambda i: (0, 0)),
            pl.BlockSpec((MOE_TT, MOE_TT), lambda i: (0, 0)),
        ],
        out_specs=[
            pl.BlockSpec((tr, D_MODEL), lambda i: (i, 0)),
            pl.BlockSpec((tr, ROUTER_PAD), lambda i: (i, 0)),
            pl.BlockSpec((ROUTE_ROWS, tr), lambda i: (0, i)),
            pl.BlockSpec((tiles, 1, ROUTER_PAD), lambda i: (i, 0, 0)),
            pl.BlockSpec((zero_rows, D_MODEL), lambda i: (i, 0)),
        ],
        out_shape=[jax.ShapeDtypeStruct((n, D_MODEL), BF16),
                   jax.ShapeDtypeStruct((n, ROUTER_PAD), F32),
                   jax.ShapeDtypeStruct((ROUTE_ROWS, n), F32),
                   jax.ShapeDtypeStruct((n // MOE_TT, 1, ROUTER_PAD), jnp.int32),
                   jax.ShapeDtypeStruct((sorted_rows, D_MODEL), BF16)],
        compiler_params=pltpu.CompilerParams(
            dimension_semantics=("arbitrary",), vmem_limit_bytes=VMEM_LIMIT_BYTES),
        name="router",
    )(x2d, g, w_hi, w_lo, b_pad, ltri)


def _routing_tables(cnt, n):
    i32 = jnp.int32
    nt = n // MOE_TT
    cnt = cnt.reshape(nt, ROUTER_PAD)[:, :N_EXPERTS]
    total = jnp.sum(cnt, axis=0)
    region = ((total + REGION_SLACK + MOE_TM - 1) // MOE_TM) * MOE_TM
    start = jnp.cumsum(region) - region
    first = start[None, :] + jnp.cumsum(cnt, axis=0) - cnt
    a = jnp.concatenate([first, (start + total)[None, :]], axis=0).reshape(-1).astype(i32)
    block_row = jnp.arange(_sorted_rows(n) // MOE_TM, dtype=i32) * MOE_TM
    ends = start + region
    block_expert = jnp.minimum(jnp.sum((ends[None, :] <= block_row[:, None]).astype(i32), axis=1), N_EXPERTS - 1)
    block_used = (block_row < (start + total)[block_expert]).astype(i32)
    fill = jnp.max(jnp.bitwise_and(first, ROW_ALIGN - 1) + cnt, axis=1)
    fast_dispatch = jnp.logical_and(fill <= DISPATCH_FAST_FILL, jnp.arange(nt) < nt - 1).astype(i32)
    fast_combine = (fill <= COMBINE_FAST_WIN).astype(i32)
    return a, block_expert, block_used, fast_dispatch, fast_combine


def _sorted_rows(n):
    return -(-(2 * n + N_EXPERTS * (REGION_SLACK + MOE_TM)) // MOE_TM) * MOE_TM


def _window(a_ref, t, e):
    a = a_ref[t * N_EXPERTS + e]
    off = jnp.bitwise_and(a, ROW_ALIGN - 1)
    return pl.multiple_of(a - off, ROW_ALIGN), off


def _expert_slot(e, e1, e2, r1, r2, off):
    d = jnp.where(e1 == e, r1, jnp.where(e2 == e, r2, -1.0))
    return jnp.where(d >= 0.0, d + off.astype(F32), -1.0).astype(jnp.int32)


def _dispatch_kernel(a_ref, fast_ref, hn_ref, rt_ref, zeros_hbm, xs_hbm, carry, stage, sem):
    del zeros_hbm
    t = pl.program_id(0)
    nt = pl.num_programs(0)
    slot = t % 2

    def window_copy(sl, e, base, win):
        return pltpu.make_async_copy(stage.at[sl, e, 0:win], xs_hbm.at[pl.ds(base, win)], sem)

    @pl.when(t == 0)
    def _():
        carry[...] = jnp.zeros(carry.shape, F32)

    def stage_and_send(win, prev_win_is_fast):
        hn = hn_ref[...]
        rt = rt_ref[...]
        e1, e2, r1, r2 = rt[0:1], rt[1:2], rt[2:3], rt[3:4]
        row = lax.broadcasted_iota(jnp.int32, (win, MOE_TT), 0)
        head_row = lax.broadcasted_iota(jnp.int32, (ROW_ALIGN, 1), 0)
        bases = []
        for e in range(N_EXPERTS):
            base, off = _window(a_ref, t, e)
            bases.append(base)
            slot_of = _expert_slot(e, e1, e2, r1, r2, off)
            sel = jnp.where(row == slot_of, 1.0, 0.0).astype(BF16)
            rows = jnp.dot(sel, hn, preferred_element_type=F32)
            head = jnp.where(head_row < off, carry[e], rows[0:ROW_ALIGN])
            stage[slot, e, 0:ROW_ALIGN, :] = head.astype(BF16)
            stage[slot, e, ROW_ALIGN:win, :] = rows[ROW_ALIGN:].astype(BF16)
            filled = off + a_ref[(t + 1) * N_EXPERTS + e] - a_ref[t * N_EXPERTS + e]
            last_group = pl.multiple_of(lax.shift_right_logical(filled, 4) * ROW_ALIGN, ROW_ALIGN)
            carry[e] = stage[slot, e, pl.ds(last_group, ROW_ALIGN), :].astype(F32)

        for was_fast, prev_win in ((1, DISPATCH_FAST_WIN), (0, MOE_WIN)):
            @pl.when(jnp.logical_and(t > 0, prev_win_is_fast == was_fast))
            def _():
                for e in range(N_EXPERTS):
                    window_copy(1 - slot, e, 0, prev_win).wait()

        for e in range(N_EXPERTS):
            window_copy(slot, e, bases[e], win).start()

    is_fast = fast_ref[t]
    prev_fast = fast_ref[jnp.maximum(t - 1, 0)]

    @pl.when(is_fast == 1)
    def _():
        stage_and_send(DISPATCH_FAST_WIN, prev_fast)

    @pl.when(is_fast == 0)
    def _():
        stage_and_send(MOE_WIN, prev_fast)

    @pl.when(t == nt - 1)
    def _():
        for e in range(N_EXPERTS):
            window_copy(slot, e, 0, MOE_WIN).wait()


def _dispatch(hn, route_t, a, fast, zeros):
    n = hn.shape[0]
    tt = MOE_TT
    grid_spec = pltpu.PrefetchScalarGridSpec(
        num_scalar_prefetch=2,
        grid=(n // tt,),
        in_specs=[
            pl.BlockSpec((tt, D_MODEL), lambda t, a, fs: (t, 0)),
            pl.BlockSpec((ROUTE_ROWS, tt), lambda t, a, fs: (0, t)),
            pl.BlockSpec(memory_space=pl.ANY),
        ],
        out_specs=pl.BlockSpec(memory_space=pl.ANY),
        scratch_shapes=[
            pltpu.VMEM((N_EXPERTS, ROW_ALIGN, D_MODEL), F32),
            pltpu.VMEM((2, N_EXPERTS, MOE_WIN, D_MODEL), BF16),
            pltpu.SemaphoreType.DMA(()),
        ],
    )
    return pl.pallas_call(
        _dispatch_kernel,
        grid_spec=grid_spec,
        out_shape=jax.ShapeDtypeStruct(zeros.shape, BF16),
        input_output_aliases={4: 0},
        compiler_params=pltpu.CompilerParams(
            dimension_semantics=("arbitrary",), vmem_limit_bytes=VMEM_LIMIT_BYTES),
        name="moe_dispatch",
    )(a, fast, hn, route_t, zeros)


def _combine_kernel(a_ref, fast_ref, x_ref, rc_ref, ys_hbm, o_ref, ybuf, yfast, sem):
    t = pl.program_id(0)
    nt = pl.num_programs(0)
    slot = t % 2
    x = x_ref[...]
    rc = rc_ref[...]
    e1, e2, r1, r2, g1, g2 = (rc[:, k:k + 1] for k in range(6))

    def wide_copy(tile, sl, e):
        base, _ = _window(a_ref, tile, e)
        return pltpu.make_async_copy(ys_hbm.at[pl.ds(base, MOE_WIN)], ybuf.at[sl, e], sem.at[sl])

    def narrow_copy(tile, sl, e):
        base, _ = _window(a_ref, tile, e)
        return pltpu.make_async_copy(ys_hbm.at[pl.ds(base, COMBINE_FAST_WIN)],
                                     yfast.at[sl, pl.ds(e * COMBINE_FAST_WIN, COMBINE_FAST_WIN)], sem.at[sl])

    def start_windows(tile, sl):
        for flag, copy in ((1, narrow_copy), (0, wide_copy)):
            @pl.when(fast_ref[tile] == flag)
            def _():
                for e in range(N_EXPERTS):
                    copy(tile, sl, e).start()

    @pl.when(t == 0)
    def _():
        start_windows(0, 0)

    @pl.when(t + 1 < nt)
    def _():
        start_windows(t + 1, 1 - slot)

    @pl.when(fast_ref[t] == 0)
    def _():
        for e in range(N_EXPERTS):
            wide_copy(t, slot, e).wait()
        lane = lax.broadcasted_iota(jnp.int32, (MOE_TT, MOE_WIN), 1)
        acc = jnp.zeros((MOE_TT, D_MODEL), F32)
        for e in range(N_EXPERTS):
            _, off = _window(a_ref, t, e)
            slot_of = _expert_slot(e, e1, e2, r1, r2, off)
            sel = jnp.where(lane == slot_of, 1.0, 0.0).astype(BF16)
            gate = jnp.where(e1 == e, g1, jnp.where(e2 == e, g2, 0.0))
            acc = acc + gate * jnp.dot(sel, ybuf[slot, e], preferred_element_type=F32)
        o_ref[...] = x + acc

    @pl.when(fast_ref[t] == 1)
    def _():
        for e in range(N_EXPERTS):
            narrow_copy(t, slot, e).wait()
        shape = (MOE_TT, COMBINE_FAST_WIN)
        lane = lax.broadcasted_iota(jnp.int32, shape, 1).astype(F32)
        offs = [_window(a_ref, t, e)[1].astype(F32) for e in range(N_EXPERTS)]
        picked = []
        for ek, rk in ((e1, r1), (e2, r2)):
            eb = jnp.broadcast_to(ek, shape)
            pos = jnp.broadcast_to(rk, shape)
            for e in range(N_EXPERTS):
                pos = pos + jnp.where(eb == e, offs[e], 0.0)
            hit = lane == pos
            sel = jnp.concatenate(
                [jnp.where(jnp.logical_and(eb == e, hit), 1.0, 0.0).astype(BF16) for e in range(N_EXPERTS)],
                axis=1)
            picked.append(jnp.dot(sel, yfast[slot], preferred_element_type=F32))
        o_ref[...] = x + (g1 * picked[0] + g2 * picked[1])


def _combine(x2d, route_c, ys, a, fast):
    n = x2d.shape[0]
    tt = MOE_TT
    grid_spec = pltpu.PrefetchScalarGridSpec(
        num_scalar_prefetch=2,
        grid=(n // tt,),
        in_specs=[
            pl.BlockSpec((tt, D_MODEL), lambda t, a, fs: (t, 0)),
            pl.BlockSpec((tt, ROUTER_PAD), lambda t, a, fs: (t, 0)),
            pl.BlockSpec(memory_space=pl.ANY),
        ],
        out_specs=pl.BlockSpec((tt, D_MODEL), lambda t, a, fs: (t, 0)),
        scratch_shapes=[
            pltpu.VMEM((2, N_EXPERTS, MOE_WIN, D_MODEL), BF16),
            pltpu.VMEM((2, N_EXPERTS * COMBINE_FAST_WIN, D_MODEL), BF16),
            pltpu.SemaphoreType.DMA((2,)),
        ],
    )
    return pl.pallas_call(
        _combine_kernel,
        grid_spec=grid_spec,
        out_shape=jax.ShapeDtypeStruct((n, D_MODEL), F32),
        compiler_params=pltpu.CompilerParams(
            dimension_semantics=("arbitrary",), vmem_limit_bytes=VMEM_LIMIT_BYTES),
        name="moe_combine",
    )(a, fast, x2d, route_c, ys)


def _swa_bias():
    qi = jnp.arange(PAIR)[:, None]
    kj = jnp.arange(WIN)[None, :]
    dist = jnp.abs(qi + HALO - kj).astype(F32)
    kc = kj // CHUNK
    qc = qi // CHUNK
    visible = jnp.logical_and(kc >= qc, kc <= qc + 2)
    slopes = jnp.asarray([2.0 ** (-8.0 * (i + 1) / SWA_HEADS) for i in range(SWA_HEADS)], F32)
    bias = -slopes[:, None, None] * dist[None] * LOG2E
    return jnp.where(visible[None], bias, -jnp.inf)


def kernel(x, mem, g_mix, w_in, g_q_swa, g_k_swa, sinks, conv_w, g_mem, w_mem_kv, g_q_mem, g_k_mem,
           g_out_swa, g_out_conv, g_out_mem, w_out, g_ffn, w_gate_dense, w_up_dense, w_down_dense,
           w_router, b_router, w_gate_moe, w_up_moe, w_down_moe):
    batch, seq, _ = x.shape
    depth = g_mix.shape[0]
    n = batch * seq
    assert seq % min(seq, 512) == 0 and min(seq, 512) % PAIR == 0, "sequence length must tile into query pairs"
    assert n % min(n, ROUTER_ROWS) == 0 and n % MOE_TM == 0, "token count must tile the routing kernels"
    scale = HEAD_DIM ** -0.5 * LOG2E

    bias = _swa_bias()
    gqk = jnp.concatenate([jnp.tile(g_q_swa * scale, (1, SWA_HEADS)),
                           jnp.tile(g_k_swa, (1, SWA_KV_HEADS))], axis=1)[:, None, :]
    gqm = jnp.tile(g_q_mem * scale, (1, MEM_HEADS))[:, None, :]
    gkm = jnp.tile(g_k_mem, (1, MEM_HEADS))[:, None, :]
    gout = jnp.concatenate([g_out_swa, g_out_conv, g_out_mem], axis=1)[:, None, :]

    w_in_b = w_in.astype(BF16)
    w_out_b = w_out.astype(BF16)
    mem_kv_all = _mem_kv(mem.reshape(-1, D_MODEL), g_mem[:, None, :], w_mem_kv.astype(BF16), gkm)

    wr = jnp.pad(w_router, ((0, 0), (0, 0), (0, ROUTER_PAD - N_EXPERTS)))
    wr_hi = wr.astype(BF16)
    wr_lo = (wr - wr_hi.astype(F32)).astype(BF16)
    br = jnp.pad(b_router, ((0, 0), (0, ROUTER_PAD - N_EXPERTS)), constant_values=-jnp.inf)[:, None, :]
    ltri = (jnp.arange(MOE_TT)[:, None] >= jnp.arange(MOE_TT)[None, :]).astype(BF16)

    wg_dense, wu_dense, wd_dense = (w.astype(BF16) for w in (w_gate_dense, w_up_dense, w_down_dense))
    n_moe, n_exp = w_gate_moe.shape[:2]
    assert depth % 2 == 0 and n_moe == depth // 2, "every expert layer must follow a dense layer"
    ex_gate = w_gate_moe.reshape(n_moe, n_exp * D_MODEL, D_FF)
    ex_up = w_up_moe.reshape(n_moe, n_exp * D_MODEL, D_FF)
    ex_down = w_down_moe.reshape(n_moe, n_exp * D_FF, D_MODEL)

    xs = x.reshape(n, D_MODEL)
    for l in range(depth):
        xs = _mixer(xs.reshape(batch, seq, D_MODEL), sinks[l] * LOG2E, g_mix[l][None], w_in_b[l], gqk[l], gqm[l],
                    bias, conv_w[l], [t[l].reshape(batch, -1, MEM_WIDTH) for t in mem_kv_all], gout[l],
                    w_out_b[l]).reshape(n, D_MODEL)
        i = l // 2
        gf = g_ffn[l][None]
        if l % 2 == 0:
            xs, wg_moe, wu_moe, wd_moe = _dense_ffn(xs, gf, i, wg_dense, wu_dense, wd_dense, ex_gate, ex_up, ex_down)
            wg_moe = wg_moe.reshape(n_exp, D_MODEL, D_FF)
            wu_moe = wu_moe.reshape(n_exp, D_MODEL, D_FF)
            wd_moe = wd_moe.reshape(n_exp, D_FF, D_MODEL)
        else:
            hn, route_c, route_t, cnt, xs_zeros = _router(xs, gf, wr_hi[i], wr_lo[i], br[i], ltri)
            a, block_expert, block_used, fast_d, fast_c = _routing_tables(cnt, n)
            x_sorted = _dispatch(hn, route_t, a, fast_d, xs_zeros)
            y_sorted = _moe_ffn(x_sorted, block_expert, block_used, wg_moe, wu_moe, wd_moe)
            xs = _combine(xs, route_c, y_sorted, a, fast_c)
    return xs.reshape(batch, seq, D_MODEL)
```
